```python
import jax, jax.numpy as jnp
from jax import lax
import numpy as np

D_MODEL = 1024
BATCH = 16
SEQ = 2048
DEPTH = 1

N_META = 16
D_MIX = D_MODEL
D_CONV = D_MIX // 2
CONV_WIDTH = 31
GLA_HEADS = 4
GLA_DV = (D_MIX - D_CONV) // GLA_HEADS
GLA_DK = GLA_DV // 2
GLA_GATE_RANK = 16
GLA_TAU = 16.0
CHUNK = 64
D_FF = 4 * D_MODEL
LN_EPS = 1e-5
DEEPNORM_ALPHA = (2.0 * DEPTH) ** 0.25
DEEPNORM_BETA = (8.0 * DEPTH) ** -0.25

SPLIT_SIZES = (D_CONV, D_CONV,
               GLA_HEADS * GLA_DK, GLA_HEADS * GLA_DK,
               GLA_HEADS * GLA_DV, GLA_HEADS * GLA_DV,
               GLA_GATE_RANK)
D_IN = sum(SPLIT_SIZES)
SPLIT_IDX = tuple(int(i) for i in np.cumsum(SPLIT_SIZES)[:-1])

kernel_name = "hymba_conformer_gla_deepnorm"


def layer_norm(x, g, b):
    xf = x.astype(jnp.float32)
    mu = jnp.mean(xf, axis=-1, keepdims=True)
    var = jnp.mean(jnp.square(xf - mu), axis=-1, keepdims=True)
    y = (xf - mu) * lax.rsqrt(var + LN_EPS)
    return (y * g.astype(jnp.float32) + b.astype(jnp.float32)).astype(x.dtype)


def rms_norm(x, g):
    xf = x.astype(jnp.float32)
    y = xf * lax.rsqrt(jnp.mean(jnp.square(xf), axis=-1, keepdims=True) + LN_EPS)
    return y * g.astype(jnp.float32)


def conformer_conv(a, gate, conv_w, conv_b, ln_g, ln_b):
    h = a * jax.nn.sigmoid(gate)
    h = lax.conv_general_dilated(
        h, conv_w[:, None, :].astype(h.dtype),
        window_strides=(1,), padding=[(CONV_WIDTH - 1, 0)],
        dimension_numbers=("NWC", "WIO", "NWC"),
        feature_group_count=D_CONV) + conv_b
    return jax.nn.silu(layer_norm(h, ln_g, ln_b))


def gla_chunked(q, k, v, log_g):
    B, T = q.shape[0], q.shape[1]
    pad = (-N_META) % CHUNK
    padw = ((0, 0), (pad, 0), (0, 0), (0, 0))
    q, k, v, log_g = [jnp.pad(t.astype(jnp.float32), padw) for t in (q, k, v, log_g)]
    L = T + pad
    N = L // CHUNK

    def to_chunks(t):
        return t.reshape(B, N, CHUNK, GLA_HEADS, t.shape[-1]).transpose(0, 3, 1, 2, 4)

    q, k, v, log_g = map(to_chunks, (q, k, v, log_g))
    q = q * (GLA_DK ** -0.5)
    b = jnp.cumsum(log_g, axis=3)
    b_last = b[:, :, :, -1:, :]
    qe = q * jnp.exp(b)
    ke = k * jnp.exp(-b)
    kd = k * jnp.exp(b_last - b)

    mask = jnp.tril(jnp.ones((CHUNK, CHUNK), dtype=bool))
    A = jnp.einsum("bhncd,bhnsd->bhncs", qe, ke)
    A = jnp.where(mask, A, 0.0)
    o_intra = jnp.einsum("bhncs,bhnse->bhnce", A, v)

    dS = jnp.einsum("bhncd,bhnce->bhnde", kd, v)
    decay = jnp.exp(b_last[:, :, :, 0, :])

    def step(S, xs):
        dec, upd = xs
        return dec[..., None] * S + upd, S

    S0 = jnp.zeros((B, GLA_HEADS, GLA_DK, GLA_DV), jnp.float32)
    _, S_before = lax.scan(step, S0, (jnp.moveaxis(decay, 2, 0), jnp.moveaxis(dS, 2, 0)))
    S_before = jnp.moveaxis(S_before, 0, 2)
    o_inter = jnp.einsum("bhncd,bhnde->bhnce", qe, S_before)

    o = (o_intra + o_inter).transpose(0, 2, 3, 1, 4).reshape(B, L, GLA_HEADS, GLA_DV)
    return o[:, pad:]


def setup_inputs(seed: int = 0) -> dict:
    key = jax.random.key(seed)
    ks = jax.random.split(key, 20)
    f32 = jnp.float32

    def nrm(k, shape, scale):
        return jax.random.normal(k, shape, f32) * scale

    def gain(k, shape):
        return 1.0 + 0.02 * jax.random.normal(k, shape, f32)

    return {
        "x": jax.random.normal(ks[0], (BATCH, SEQ, D_MODEL), f32),
        "meta_tokens": nrm(ks[1], (N_META, D_MODEL), 1.0),
        "ln_in_g": gain(ks[2], (D_MODEL,)),
        "ln_in_b": nrm(ks[3], (D_MODEL,), 0.02),
        "w_in": nrm(ks[4], (DEPTH, D_MODEL, D_IN), D_MODEL ** -0.5),
        "conv_w": nrm(ks[5], (DEPTH, CONV_WIDTH, D_CONV), CONV_WIDTH ** -0.5),
        "conv_b": nrm(ks[6], (DEPTH, D_CONV), 0.02),
        "conv_ln_g": gain(ks[7], (DEPTH, D_CONV)),
        "conv_ln_b": nrm(ks[8], (DEPTH, D_CONV), 0.02),
        "gate_up": nrm(ks[9], (DEPTH, GLA_GATE_RANK, GLA_HEADS * GLA_DK), GLA_GATE_RANK ** -0.5),
        "gate_bias": nrm(ks[10], (DEPTH, GLA_HEADS * GLA_DK), 0.02),
        "gla_norm_g": gain(ks[11], (DEPTH, GLA_DV)),
        "w_out": nrm(ks[12], (DEPTH, D_MIX, D_MODEL), DEEPNORM_BETA * D_MIX ** -0.5),
        "ln1_g": gain(ks[13], (DEPTH, D_MODEL)),
        "ln1_b": nrm(ks[14], (DEPTH, D_MODEL), 0.02),
        "w_ff1": nrm(ks[15], (DEPTH, D_MODEL, D_FF), D_MODEL ** -0.5),
        "w_ff2": nrm(ks[16], (DEPTH, D_FF, D_MODEL), DEEPNORM_BETA * D_FF ** -0.5),
        "ln2_g": gain(ks[17], (DEPTH, D_MODEL)),
        "ln2_b": nrm(ks[18], (DEPTH, D_MODEL), 0.02),
    }


def reference(x, meta_tokens, ln_in_g, ln_in_b, w_in, conv_w, conv_b, conv_ln_g, conv_ln_b,
              gate_up, gate_bias, gla_norm_g, w_out, ln1_g, ln1_b, w_ff1, w_ff2, ln2_g, ln2_b):
    B = x.shape[0]
    meta = jnp.broadcast_to(meta_tokens.astype(x.dtype)[None], (B, N_META, D_MODEL))
    s = jnp.concatenate([meta, x], axis=1)
    s = layer_norm(s, ln_in_g, ln_in_b)
    T = s.shape[1]

    for l in range(DEPTH):
        u = s @ w_in[l]
        c_val, c_gate, q, k, v, r, g_down = jnp.split(u, SPLIT_IDX, axis=-1)

        conv_out = conformer_conv(c_val, c_gate, conv_w[l], conv_b[l], conv_ln_g[l], conv_ln_b[l])

        z = (g_down @ gate_up[l] + gate_bias[l]).astype(jnp.float32)
        log_g = jax.nn.log_sigmoid(z) / GLA_TAU
        hd = lambda t, d: t.reshape(B, T, GLA_HEADS, d)
        o = gla_chunked(hd(q, GLA_DK), hd(k, GLA_DK), hd(v, GLA_DV), hd(log_g, GLA_DK))
        o = rms_norm(o, gla_norm_g[l]) * jax.nn.silu(hd(r, GLA_DV).astype(jnp.float32))
        gla_out = o.reshape(B, T, GLA_HEADS * GLA_DV).astype(s.dtype)

        mix = jnp.concatenate([conv_out, gla_out], axis=-1) @ w_out[l]
        s = layer_norm(DEEPNORM_ALPHA * s + mix, ln1_g[l], ln1_b[l])

        f = jnp.square(jax.nn.relu(s @ w_ff1[l])) @ w_ff2[l]
        s = layer_norm(DEEPNORM_ALPHA * s + f, ln2_g[l], ln2_b[l])

    return s[:, N_META:]
```

```python
import functools

import jax
import jax.numpy as jnp
from jax import lax
from jax.experimental import pallas as pl
from jax.experimental.pallas import tpu as pltpu

D_MODEL = 1024
N_META = 16
D_CONV = 512
CONV_WIDTH = 31
GLA_HEADS = 4
GLA_DV = 128
GLA_DK = 64
GLA_GATE_RANK = 16
GLA_TAU = 16.0
CHUNK = 64
D_FF = 4096
LN_EPS = 1e-5
DEPTH = 1
DEEPNORM_ALPHA = (2.0 * DEPTH) ** 0.25

D_QK = GLA_HEADS * GLA_DK
D_V = GLA_HEADS * GLA_DV
D_MAIN = 2 * D_CONV + 2 * D_QK + 2 * D_V
LANES = 128
HALO = 32
TILE_M = 256
FF_CHUNK = 1024
VMEM_LIMIT_BYTES = 56 * 1024 * 1024

F32 = jnp.float32
BF16 = jnp.bfloat16


def _dot(a, b):
    return jnp.dot(a, b, preferred_element_type=F32)


def _dot_t0(a, b):
    return lax.dot_general(a, b, (((0,), (0,)), ((), ())), preferred_element_type=F32)


def _dot_t1(a, b):
    return lax.dot_general(a, b, (((1,), (1,)), ((), ())), preferred_element_type=F32)


def _layer_norm(x, g, b):
    mu = jnp.mean(x, axis=-1, keepdims=True)
    xc = x - mu
    var = jnp.mean(xc * xc, axis=-1, keepdims=True)
    return xc * lax.rsqrt(var + LN_EPS) * g + b


def _sigmoid(x):
    return 1.0 / (1.0 + jnp.exp(-x))


def _silu(x):
    return x * _sigmoid(x)


def _log_sigmoid(z):
    return jnp.minimum(z, 0.0) - jnp.log(1.0 + jnp.exp(-jnp.abs(z)))


def _split_hi_lo(x):
    hi = x.astype(BF16)
    lo = (x - hi.astype(F32)).astype(BF16)
    return hi, lo


def _chunk_cumsum(lg, rows):
    ri = lax.broadcasted_iota(jnp.int32, (rows, rows), 0)
    ci = lax.broadcasted_iota(jnp.int32, (rows, rows), 1)
    tri = jnp.where((ri // CHUNK == ci // CHUNK) & (ci <= ri), 1.0, 0.0).astype(BF16)
    hi, lo = _split_hi_lo(lg)
    return _dot(tri, hi) + _dot(tri, lo)


def _chunk_total_cols(lg, rows):
    n_chunks = rows // CHUNK
    ri = lax.broadcasted_iota(jnp.int32, (rows, n_chunks * LANES), 0)
    ci = lax.broadcasted_iota(jnp.int32, (rows, n_chunks * LANES), 1)
    ind = jnp.where(ri // CHUNK == ci // LANES, 1.0, 0.0).astype(BF16)
    hi, lo = _split_hi_lo(lg)
    return _dot_t0(hi, ind) + _dot_t0(lo, ind)


def _state_delta(kd_c, v_c):
    full = _dot_t0(kd_c.astype(BF16), v_c.astype(BF16))
    return jnp.concatenate(
        [full[h * GLA_DK:(h + 1) * GLA_DK, h * GLA_DV:(h + 1) * GLA_DV] for h in range(GLA_HEADS)],
        axis=0)


def _in_projection(s, wmain_ref, wgd_ref, gup_ref, gbias_ref):
    sb = s.astype(BF16)
    u = _dot(sb, wmain_ref[...])
    gd = _dot(sb, wgd_ref[...])
    z = _dot(gd.astype(BF16), gup_ref[...]) + gbias_ref[...]
    lg = _log_sigmoid(z) * (1.0 / GLA_TAU)
    return u, lg


def _meta_kernel(meta_ref, lng_ref, lnb_ref, wmain_ref, wgd_ref, gup_ref, gbias_ref,
                 hmeta_ref, smeta_ref):
    pad = CHUNK - N_META
    s = _layer_norm(meta_ref[...], lng_ref[...], lnb_ref[...])
    s = jnp.concatenate([jnp.zeros((pad, D_MODEL), F32), s], axis=0)
    u, lg = _in_projection(s, wmain_ref, wgd_ref, gup_ref, gbias_ref)
    h = u[:, :D_CONV] * _sigmoid(u[:, D_CONV:2 * D_CONV])
    hmeta_ref[...] = h[pad:, :]
    is_meta = lax.broadcasted_iota(jnp.int32, (CHUNK, 1), 0) >= pad
    k = jnp.where(is_meta, u[:, 2 * D_CONV + D_QK:2 * D_CONV + 2 * D_QK], 0.0)
    v = jnp.where(is_meta, u[:, 2 * D_CONV + 2 * D_QK:2 * D_CONV + 2 * D_QK + D_V], 0.0)
    lg = jnp.where(is_meta, lg, 0.0)
    b = _chunk_cumsum(lg, CHUNK)
    kd = k * jnp.exp(b[CHUNK - 1:CHUNK, :] - b)
    smeta_ref[...] = _state_delta(kd, v)


def _layer_kernel(x_ref, hmeta_ref, smeta_ref, lng_ref, lnb_ref, wmain_ref, wgd_ref, gup_ref,
                  gbias_ref, cw_ref, cb_ref, clg_ref, clb_ref, gng_ref, wout_ref,
                  l1g_ref, l1b_ref, w1_ref, w2_ref, l2g_ref, l2b_ref,
                  o_ref, hbuf, sbuf):
    tm = TILE_M
    n_chunks = tm // CHUNK

    @pl.when(pl.program_id(1) == 0)
    def _():
        hbuf[0:HALO - N_META, :] = jnp.zeros((HALO - N_META, D_CONV), F32)
        hbuf[HALO - N_META:HALO, :] = hmeta_ref[...]
        sbuf[...] = smeta_ref[...]

    s = _layer_norm(x_ref[...], lng_ref[...], lnb_ref[...])
    u, lg = _in_projection(s, wmain_ref, wgd_ref, gup_ref, gbias_ref)
    q = u[:, 2 * D_CONV:2 * D_CONV + D_QK]
    k = u[:, 2 * D_CONV + D_QK:2 * D_CONV + 2 * D_QK]
    v = u[:, 2 * D_CONV + 2 * D_QK:2 * D_CONV + 2 * D_QK + D_V]
    r = u[:, 2 * D_CONV + 2 * D_QK + D_V:D_MAIN]

    hbuf[HALO:HALO + tm, :] = u[:, :D_CONV] * _sigmoid(u[:, D_CONV:2 * D_CONV])
    conv_blocks = []
    for r0 in range(0, tm, CHUNK):
        acc = jnp.zeros((CHUNK, D_CONV), F32) + cb_ref[...]
        for j in range(CONV_WIDTH):
            start = HALO - (CONV_WIDTH - 1) + j + r0
            acc = acc + hbuf[start:start + CHUNK, :] * cw_ref[j:j + 1, :]
        conv_blocks.append(acc)
    hbuf[0:HALO, :] = hbuf[tm:tm + HALO, :]
    conv_out = _silu(_layer_norm(jnp.concatenate(conv_blocks, axis=0), clg_ref[...], clb_ref[...]))

    b = _chunk_cumsum(lg, tm)
    total_cols = _chunk_total_cols(lg, tm)
    qe = q * (GLA_DK ** -0.5) * jnp.exp(b)
    ke = (k * jnp.exp(-b)).astype(BF16)
    vb = v.astype(BF16)
    ri = lax.broadcasted_iota(jnp.int32, (tm, tm), 0)
    ci = lax.broadcasted_iota(jnp.int32, (tm, tm), 1)
    causal = (ri // CHUNK == ci // CHUNK) & (ci <= ri)
    lane_head = lax.broadcasted_iota(jnp.int32, (1, D_QK), 1) // GLA_DK
    qe_heads = [jnp.where(lane_head == h, qe, 0.0).astype(BF16) for h in range(GLA_HEADS)]

    o_heads = []
    for h in range(GLA_HEADS):
        a = jnp.where(causal, _dot_t1(qe_heads[h], ke), 0.0)
        o_heads.append(_dot(a.astype(BF16), vb[:, h * GLA_DV:(h + 1) * GLA_DV]))

    o_inter = [[] for _ in range(GLA_HEADS)]
    state = sbuf[...]
    for c in range(n_chunks):
        rows = slice(c * CHUNK, (c + 1) * CHUNK)
        state_b = state.astype(BF16)
        for h in range(GLA_HEADS):
            o_inter[h].append(_dot(qe_heads[h][rows, :], state_b))
        b_c = b[rows, :]
        kd = k[rows, :] * jnp.exp(b_c[CHUNK - 1:CHUNK, :] - b_c)
        decay = jnp.exp(total_cols[:, c * LANES:(c + 1) * LANES])
        state = decay * state + _state_delta(kd, v[rows, :])
    sbuf[...] = state

    gla_blocks = []
    for h in range(GLA_HEADS):
        o = o_heads[h] + jnp.concatenate(o_inter[h], axis=0)
        o = o * lax.rsqrt(jnp.mean(o * o, axis=-1, keepdims=True) + LN_EPS) * gng_ref[...]
        gla_blocks.append(o * _silu(r[:, h * GLA_DV:(h + 1) * GLA_DV]))

    mixin = jnp.concatenate([conv_out] + gla_blocks, axis=-1).astype(BF16)
    s1 = _layer_norm(DEEPNORM_ALPHA * s + _dot(mixin, wout_ref[...]), l1g_ref[...], l1b_ref[...])

    s1b = s1.astype(BF16)
    f = jnp.zeros((tm, D_MODEL), F32)
    for c0 in range(0, D_FF, FF_CHUNK):
        hid = jnp.maximum(_dot(s1b, w1_ref[:, c0:c0 + FF_CHUNK]), 0.0)
        f = f + _dot((hid * hid).astype(BF16), w2_ref[c0:c0 + FF_CHUNK, :])
    o_ref[...] = _layer_norm(DEEPNORM_ALPHA * s1 + f, l2g_ref[...], l2b_ref[...])


def _resident(shape):
    return pl.BlockSpec(shape, lambda *_: (0,) * len(shape), pipeline_mode=pl.Buffered(1))


def kernel(x, meta_tokens, ln_in_g, ln_in_b, w_in, conv_w, conv_b, conv_ln_g, conv_ln_b,
           gate_up, gate_bias, gla_norm_g, w_out, ln1_g, ln1_b, w_ff1, w_ff2, ln2_g, ln2_b):
    batch, seq, d_model = x.shape
    assert d_model == D_MODEL and seq % TILE_M == 0 and w_in.shape[0] == DEPTH
    row = lambda a: a.reshape(1, -1).astype(F32)

    w_main = w_in[0, :, :D_MAIN].astype(BF16)
    w_gd = jnp.pad(w_in[0, :, D_MAIN:], ((0, 0), (0, LANES - GLA_GATE_RANK))).astype(BF16)
    gup = jnp.pad(gate_up[0], ((0, LANES - GLA_GATE_RANK), (0, 0))).astype(BF16)
    lng, lnb, gbias = row(ln_in_g), row(ln_in_b), row(gate_bias[0])

    proj_args = (lng, lnb, w_main, w_gd, gup, gbias)
    h_meta, s_meta = pl.pallas_call(
        _meta_kernel,
        out_shape=(jax.ShapeDtypeStruct((N_META, D_CONV), F32),
                   jax.ShapeDtypeStruct((D_QK, GLA_DV), F32)),
        compiler_params=pltpu.CompilerParams(vmem_limit_bytes=VMEM_LIMIT_BYTES),
        name="meta_state",
    )(meta_tokens.astype(F32), *proj_args)

    layer_args = (
        h_meta, s_meta, *proj_args,
        conv_w[0].astype(F32), row(conv_b[0]), row(conv_ln_g[0]), row(conv_ln_b[0]),
        row(gla_norm_g[0]), w_out[0].astype(BF16), row(ln1_g[0]), row(ln1_b[0]),
        w_ff1[0].astype(BF16), w_ff2[0].astype(BF16), row(ln2_g[0]), row(ln2_b[0]),
    )
    x_spec = pl.BlockSpec((None, TILE_M, D_MODEL), lambda b, t: (b, t, 0))
    return pl.pallas_call(
        _layer_kernel,
        grid=(batch, seq // TILE_M),
        in_specs=[x_spec] + [_resident(a.shape) for a in layer_args],
        out_specs=x_spec,
        out_shape=jax.ShapeDtypeStruct(x.shape, x.dtype),
        scratch_shapes=[pltpu.VMEM((HALO + TILE_M, D_CONV), F32),
                        pltpu.VMEM((D_QK, GLA_DV), F32)],
        compiler_params=pltpu.CompilerParams(
            dimension_semantics=("arbitrary", "arbitrary"),
            vmem_limit_bytes=VMEM_LIMIT_BYTES),
        name="layer",
    )(x, *layer_args)
```

```python
import functools

import jax
import jax.numpy as jnp
from jax import lax
from jax.experimental import pallas as pl
from jax.experimental.pallas import tpu as pltpu

D_MODEL = 1024
N_META = 16
D_CONV = 512
CONV_WIDTH = 31
GLA_HEADS = 4
GLA_DV = 128
GLA_DK = 64
GLA_GATE_RANK = 16
GLA_TAU = 16.0
CHUNK = 64
D_FF = 4096
LN_EPS = 1e-5
DEPTH = 1
DEEPNORM_ALPHA = (2.0 * DEPTH) ** 0.25

D_QK = GLA_HEADS * GLA_DK
D_V = GLA_HEADS * GLA_DV
D_MAIN = 2 * D_CONV + 2 * D_QK + 2 * D_V
LANES = 128
SUBLANES = 8
HALO = 32
CONV_BLOCK = 128
TILE_M = 256
VMEM_LIMIT_BYTES = 56 * 1024 * 1024

F32 = jnp.float32
BF16 = jnp.bfloat16


def _dot(a, b):
    return jnp.dot(a, b, preferred_element_type=F32)


def _dot_t0(a, b):
    return lax.dot_general(a, b, (((0,), (0,)), ((), ())), preferred_element_type=F32)


def _dot_t1(a, b):
    return lax.dot_general(a, b, (((1,), (1,)), ((), ())), preferred_element_type=F32)


def _layer_norm(x, g, b):
    mu = jnp.mean(x, axis=-1, keepdims=True)
    xc = x - mu
    var = jnp.mean(xc * xc, axis=-1, keepdims=True)
    return xc * lax.rsqrt(var + LN_EPS) * g + b


def _sigmoid(x):
    return 1.0 / (1.0 + jnp.exp(-x))


def _silu(x):
    return x * _sigmoid(x)


def _log_sigmoid(z):
    return jnp.minimum(z, 0.0) - jnp.log(1.0 + jnp.exp(-jnp.abs(z)))


def _split_hi_lo(x):
    hi = x.astype(BF16)
    lo = (x - hi.astype(F32)).astype(BF16)
    return hi, lo


def _chunk_cumsum(lg, rows):
    ri = lax.broadcasted_iota(jnp.int32, (rows, rows), 0)
    ci = lax.broadcasted_iota(jnp.int32, (rows, rows), 1)
    tri = jnp.where((ri // CHUNK == ci // CHUNK) & (ci <= ri), 1.0, 0.0).astype(BF16)
    hi, lo = _split_hi_lo(lg)
    return _dot(tri, hi) + _dot(tri, lo)


def _chunk_total_cols(lg, rows):
    n_chunks = rows // CHUNK
    ri = lax.broadcasted_iota(jnp.int32, (rows, n_chunks * LANES), 0)
    ci = lax.broadcasted_iota(jnp.int32, (rows, n_chunks * LANES), 1)
    ind = jnp.where(ri // CHUNK == ci // LANES, 1.0, 0.0).astype(BF16)
    hi, lo = _split_hi_lo(lg)
    return _dot_t0(hi, ind) + _dot_t0(lo, ind)


def _state_delta(kd_c, v_c):
    full = _dot_t0(kd_c.astype(BF16), v_c.astype(BF16))
    return jnp.concatenate(
        [full[h * GLA_DK:(h + 1) * GLA_DK, h * GLA_DV:(h + 1) * GLA_DV] for h in range(GLA_HEADS)],
        axis=0)


def _causal_conv_block(hbuf, cw_ref, r0):
    shift0 = HALO - (CONV_WIDTH - 1)
    out = None
    for phase in range(SUBLANES):
        rows = CONV_BLOCK + (SUBLANES if phase else 0)
        group = None
        for base in range(0, HALO + SUBLANES, SUBLANES):
            tap = base + phase - shift0
            if 0 <= tap < CONV_WIDTH:
                term = hbuf[r0 + base:r0 + base + rows, :] * cw_ref[tap:tap + 1, :]
                group = term if group is None else group + term
        piece = group[phase:phase + CONV_BLOCK, :]
        out = piece if out is None else out + piece
    return out


def _in_projection(s, wmain_ref, wgd_ref, gup_ref, gbias_ref):
    sb = s.astype(BF16)
    u = _dot(sb, wmain_ref[...])
    gd = _dot(sb, wgd_ref[...])
    z = _dot(gd.astype(BF16), gup_ref[...]) + gbias_ref[...]
    lg = _log_sigmoid(z) * (1.0 / GLA_TAU)
    return u, lg


def _meta_kernel(meta_ref, lng_ref, lnb_ref, wmain_ref, wgd_ref, gup_ref, gbias_ref,
                 hmeta_ref, smeta_ref):
    pad = CHUNK - N_META
    s = _layer_norm(meta_ref[...], lng_ref[...], lnb_ref[...])
    s = jnp.concatenate([jnp.zeros((pad, D_MODEL), F32), s], axis=0)
    u, lg = _in_projection(s, wmain_ref, wgd_ref, gup_ref, gbias_ref)
    h = u[:, :D_CONV] * _sigmoid(u[:, D_CONV:2 * D_CONV])
    hmeta_ref[...] = h[pad:, :]
    is_meta = lax.broadcasted_iota(jnp.int32, (CHUNK, 1), 0) >= pad
    k = jnp.where(is_meta, u[:, 2 * D_CONV + D_QK:2 * D_CONV + 2 * D_QK], 0.0)
    v = jnp.where(is_meta, u[:, 2 * D_CONV + 2 * D_QK:2 * D_CONV + 2 * D_QK + D_V], 0.0)
    lg = jnp.where(is_meta, lg, 0.0)
    b = _chunk_cumsum(lg, CHUNK)
    kd = k * jnp.exp(b[CHUNK - 1:CHUNK, :] - b)
    smeta_ref[...] = _state_delta(kd, v)


def _layer_kernel(x_ref, hmeta_ref, smeta_ref, lng_ref, lnb_ref, wmain_ref, wgd_ref, gup_ref,
                  gbias_ref, cw_ref, cb_ref, clg_ref, clb_ref, gng_ref, wout_ref,
                  l1g_ref, l1b_ref, w1_ref, w2_ref, l2g_ref, l2b_ref,
                  o_ref, hbuf, sbuf, s1buf, *, tiles_per_seq, n_tiles):
    tm = TILE_M
    n_chunks = tm // CHUNK
    step = pl.program_id(0)

    @pl.when(step == 0)
    def _():
        s1buf[...] = jnp.zeros((tm, D_MODEL), F32)

    @pl.when(jnp.minimum(step, n_tiles - 1) % tiles_per_seq == 0)
    def _():
        hbuf[0:HALO - N_META, :] = jnp.zeros((HALO - N_META, D_CONV), F32)
        hbuf[HALO - N_META:HALO, :] = hmeta_ref[...]
        sbuf[...] = smeta_ref[...]

    s1_prev = s1buf[...]
    s = _layer_norm(x_ref[...], lng_ref[...], lnb_ref[...])
    hid = jnp.maximum(_dot(s1_prev.astype(BF16), w1_ref[...]), 0.0)
    u, lg = _in_projection(s, wmain_ref, wgd_ref, gup_ref, gbias_ref)
    q = u[:, 2 * D_CONV:2 * D_CONV + D_QK]
    k = u[:, 2 * D_CONV + D_QK:2 * D_CONV + 2 * D_QK]
    v = u[:, 2 * D_CONV + 2 * D_QK:2 * D_CONV + 2 * D_QK + D_V]
    r = u[:, 2 * D_CONV + 2 * D_QK + D_V:D_MAIN]
    f = _dot((hid * hid).astype(BF16), w2_ref[...])

    hbuf[HALO:HALO + tm, :] = u[:, :D_CONV] * _sigmoid(u[:, D_CONV:2 * D_CONV])
    conv = jnp.concatenate([_causal_conv_block(hbuf, cw_ref, r0) for r0 in range(0, tm, CONV_BLOCK)],
                           axis=0) + cb_ref[...]
    hbuf[0:HALO, :] = hbuf[tm:tm + HALO, :]
    conv_out = _silu(_layer_norm(conv, clg_ref[...], clb_ref[...]))

    b = _chunk_cumsum(lg, tm)
    total_cols = _chunk_total_cols(lg, tm)
    qe = q * (GLA_DK ** -0.5) * jnp.exp(b)
    ke = (k * jnp.exp(-b)).astype(BF16)
    vb = v.astype(BF16)
    ri = lax.broadcasted_iota(jnp.int32, (tm, tm), 0)
    ci = lax.broadcasted_iota(jnp.int32, (tm, tm), 1)
    causal = (ri // CHUNK == ci // CHUNK) & (ci <= ri)
    lane_head = lax.broadcasted_iota(jnp.int32, (1, D_QK), 1) // GLA_DK
    qe_heads = [jnp.where(lane_head == h, qe, 0.0).astype(BF16) for h in range(GLA_HEADS)]

    o_heads = []
    for h in range(GLA_HEADS):
        a = jnp.where(causal, _dot_t1(qe_heads[h], ke), 0.0)
        o_heads.append(_dot(a.astype(BF16), vb[:, h * GLA_DV:(h + 1) * GLA_DV]))

    o_inter = [[] for _ in range(GLA_HEADS)]
    state = sbuf[...]
    for c in range(n_chunks):
        rows = slice(c * CHUNK, (c + 1) * CHUNK)
        state_b = state.astype(BF16)
        for h in range(GLA_HEADS):
            o_inter[h].append(_dot(qe_heads[h][rows, :], state_b))
        b_c = b[rows, :]
        kd = k[rows, :] * jnp.exp(b_c[CHUNK - 1:CHUNK, :] - b_c)
        decay = jnp.exp(total_cols[:, c * LANES:(c + 1) * LANES])
        state = decay * state + _state_delta(kd, v[rows, :])
    sbuf[...] = state

    gla_blocks = []
    for h in range(GLA_HEADS):
        o = o_heads[h] + jnp.concatenate(o_inter[h], axis=0)
        o = o * lax.rsqrt(jnp.mean(o * o, axis=-1, keepdims=True) + LN_EPS) * gng_ref[...]
        gla_blocks.append(o * _silu(r[:, h * GLA_DV:(h + 1) * GLA_DV]))

    o_ref[...] = _layer_norm(DEEPNORM_ALPHA * s1_prev + f, l2g_ref[...], l2b_ref[...])

    mixin = jnp.concatenate([conv_out] + gla_blocks, axis=-1).astype(BF16)
    s1buf[...] = _layer_norm(DEEPNORM_ALPHA * s + _dot(mixin, wout_ref[...]),
                             l1g_ref[...], l1b_ref[...])


def _resident(shape):
    return pl.BlockSpec(shape, lambda *_: (0,) * len(shape), pipeline_mode=pl.Buffered(1))


def kernel(x, meta_tokens, ln_in_g, ln_in_b, w_in, conv_w, conv_b, conv_ln_g, conv_ln_b,
           gate_up, gate_bias, gla_norm_g, w_out, ln1_g, ln1_b, w_ff1, w_ff2, ln2_g, ln2_b):
    batch, seq, d_model = x.shape
    assert d_model == D_MODEL and seq % TILE_M == 0 and w_in.shape[0] == DEPTH
    row = lambda a: a.reshape(1, -1).astype(F32)

    w_main = w_in[0, :, :D_MAIN].astype(BF16)
    w_gd = jnp.pad(w_in[0, :, D_MAIN:], ((0, 0), (0, LANES - GLA_GATE_RANK))).astype(BF16)
    gup = jnp.pad(gate_up[0], ((0, LANES - GLA_GATE_RANK), (0, 0))).astype(BF16)
    lng, lnb, gbias = row(ln_in_g), row(ln_in_b), row(gate_bias[0])

    proj_args = (lng, lnb, w_main, w_gd, gup, gbias)
    h_meta, s_meta = pl.pallas_call(
        _meta_kernel,
        out_shape=(jax.ShapeDtypeStruct((N_META, D_CONV), F32),
                   jax.ShapeDtypeStruct((D_QK, GLA_DV), F32)),
        compiler_params=pltpu.CompilerParams(vmem_limit_bytes=VMEM_LIMIT_BYTES),
        name="meta_state",
    )(meta_tokens.astype(F32), *proj_args)

    layer_args = (
        h_meta, s_meta, *proj_args,
        conv_w[0].astype(F32), row(conv_b[0]), row(conv_ln_g[0]), row(conv_ln_b[0]),
        row(gla_norm_g[0]), w_out[0].astype(BF16), row(ln1_g[0]), row(ln1_b[0]),
        w_ff1[0].astype(BF16), w_ff2[0].astype(BF16), row(ln2_g[0]), row(ln2_b[0]),
    )
    tiles_per_seq = seq // TILE_M
    n_tiles = batch * tiles_per_seq

    def in_tile(i):
        j = jnp.minimum(i, n_tiles - 1)
        return (j // tiles_per_seq, j % tiles_per_seq, 0)

    def out_tile(i):
        j = jnp.maximum(i - 1, 0)
        return (j // tiles_per_seq, j % tiles_per_seq, 0)

    return pl.pallas_call(
        functools.partial(_layer_kernel, tiles_per_seq=tiles_per_seq, n_tiles=n_tiles),
        grid=(n_tiles + 1,),
        in_specs=[pl.BlockSpec((None, TILE_M, D_MODEL), in_tile)]
        + [_resident(a.shape) for a in layer_args],
        out_specs=pl.BlockSpec((None, TILE_M, D_MODEL), out_tile),
        out_shape=jax.ShapeDtypeStruct(x.shape, x.dtype),
        scratch_shapes=[pltpu.VMEM((HALO + TILE_M, D_CONV), F32),
                        pltpu.VMEM((D_QK, GLA_DV), F32),
                        pltpu.VMEM((TILE_M, D_MODEL), F32)],
        compiler_params=pltpu.CompilerParams(
            dimension_semantics=("arbitrary",),
            vmem_limit_bytes=VMEM_LIMIT_BYTES),
        name="layer",
    )(x, *layer_args)
```

```python
import functools

import jax
import jax.numpy as jnp
from jax import lax
from jax.experimental import pallas as pl
from jax.experimental.pallas import tpu as pltpu

D_MODEL = 1024
N_META = 16
D_CONV = 512
CONV_WIDTH = 31
GLA_HEADS = 4
GLA_DV = 128
GLA_DK = 64
GLA_GATE_RANK = 16
GLA_TAU = 16.0
CHUNK = 64
D_FF = 4096
LN_EPS = 1e-5
DEPTH = 1
DEEPNORM_ALPHA = (2.0 * DEPTH) ** 0.25

D_QK = GLA_HEADS * GLA_DK
D_V = GLA_HEADS * GLA_DV
D_MAIN = 2 * D_CONV + 2 * D_QK + 2 * D_V
LANES = 128
SUBLANES = 8
HALO = 32
W_COLS = 512
PIPE_DEPTH = 2
TILE_M = 256
VMEM_LIMIT_BYTES = 56 * 1024 * 1024

F32 = jnp.float32
BF16 = jnp.bfloat16


def _dot(a, b):
    return jnp.dot(a, b, preferred_element_type=F32)


def _dot_t0(a, b):
    return lax.dot_general(a, b, (((0,), (0,)), ((), ())), preferred_element_type=F32)


def _dot_t1(a, b):
    return lax.dot_general(a, b, (((1,), (1,)), ((), ())), preferred_element_type=F32)


def _layer_norm(x, g, b):
    mu = jnp.mean(x, axis=-1, keepdims=True)
    xc = x - mu
    var = jnp.mean(xc * xc, axis=-1, keepdims=True)
    return xc * lax.rsqrt(var + LN_EPS) * g + b


def _sigmoid(x):
    return 1.0 / (1.0 + jnp.exp(-x))


def _silu(x):
    return x * _sigmoid(x)


def _log_sigmoid(z):
    return jnp.minimum(z, 0.0) - jnp.log(1.0 + jnp.exp(-jnp.abs(z)))


def _zero_after(x):
    bits = lax.bitcast_convert_type(x, jnp.uint32)
    half = jnp.uint32(16)
    zero_bits = lax.shift_right_logical(lax.shift_right_logical(bits, half), half)
    return lax.bitcast_convert_type(zero_bits, F32)


def _split_hi_lo(x):
    hi = x.astype(BF16)
    lo = (x - hi.astype(F32)).astype(BF16)
    return hi, lo


def _chunk_cumsum(lg, rows):
    ri = lax.broadcasted_iota(jnp.int32, (rows, rows), 0)
    ci = lax.broadcasted_iota(jnp.int32, (rows, rows), 1)
    tri = jnp.where((ri // CHUNK == ci // CHUNK) & (ci <= ri), 1.0, 0.0).astype(BF16)
    hi, lo = _split_hi_lo(lg)
    return _dot(tri, hi) + _dot(tri, lo)


def _chunk_total_cols(lg, rows):
    n_chunks = rows // CHUNK
    ri = lax.broadcasted_iota(jnp.int32, (rows, n_chunks * LANES), 0)
    ci = lax.broadcasted_iota(jnp.int32, (rows, n_chunks * LANES), 1)
    ind = jnp.where(ri // CHUNK == ci // LANES, 1.0, 0.0).astype(BF16)
    hi, lo = _split_hi_lo(lg)
    return _dot_t0(hi, ind) + _dot_t0(lo, ind)


def _state_delta(kd_c, v_c):
    full = _dot_t0(kd_c.astype(BF16), v_c.astype(BF16))
    return jnp.concatenate(
        [full[h * GLA_DK:(h + 1) * GLA_DK, h * GLA_DV:(h + 1) * GLA_DV] for h in range(GLA_HEADS)],
        axis=0)


def _causal_conv(hbuf, cw_ref, n_rows):
    s0 = HALO - (CONV_WIDTH - 1)
    n_groups = n_rows // SUBLANES
    sub = lax.broadcasted_iota(jnp.int32, (SUBLANES, LANES), 0)
    n_base = (CONV_WIDTH + s0 - 2) // SUBLANES + 1

    def taps(phase):
        return [(a, SUBLANES * a + phase - s0) for a in range(n_base + 1)
                if 0 <= SUBLANES * a + phase - s0 < CONV_WIDTH]

    def piece(p, lanes):
        return hbuf[p * SUBLANES:(p + 1) * SUBLANES, lanes]

    def phase_sum(phase, pieces, first):
        acc = None
        for a, tap in taps(phase):
            term = pieces[a - first] * cw_ref[tap, :, lanes_of[0]]
            acc = term if acc is None else acc + term
        return acc

    strips = []
    chain = None
    lanes_of = [None]
    for lt in range(D_CONV // LANES):
        lanes = slice(lt * LANES, (lt + 1) * LANES)
        lanes_of[0] = lanes
        first_pieces = [piece(a, lanes) for a in range(n_base)]
        rolled = {r: pltpu.roll(phase_sum(r, first_pieces, 0), SUBLANES - r, 0)
                  for r in range(1, SUBLANES)}
        outs = []
        for i in range(n_groups):
            pieces = [piece(i + 1 + a, lanes) for a in range(n_base)]
            if chain is not None:
                pieces[-1] = pieces[-1] + chain
            out = phase_sum(0, pieces, 1)
            for r in range(1, SUBLANES):
                nxt = pltpu.roll(phase_sum(r, pieces, 0), SUBLANES - r, 0)
                out = out + jnp.where(sub < SUBLANES - r, rolled[r], nxt)
                rolled[r] = nxt
            outs.append(out)
            chain = _zero_after(out)
        strips.append(jnp.concatenate(outs, axis=0))
    return jnp.concatenate(strips, axis=1)


def _in_projection(s, wmain_ref, wgd_ref, gup_ref, gbias_ref):
    sb = s.astype(BF16)
    u = _dot(sb, wmain_ref[...])
    gd = _dot(sb, wgd_ref[...])
    z = _dot(gd.astype(BF16), gup_ref[...]) + gbias_ref[...]
    lg = _log_sigmoid(z) * (1.0 / GLA_TAU)
    return u, lg


def _meta_kernel(meta_ref, lng_ref, lnb_ref, wmain_ref, wgd_ref, gup_ref, gbias_ref,
                 hmeta_ref, smeta_ref):
    pad = CHUNK - N_META
    s = _layer_norm(meta_ref[...], lng_ref[...], lnb_ref[...])
    s = jnp.concatenate([jnp.zeros((pad, D_MODEL), F32), s], axis=0)
    u, lg = _in_projection(s, wmain_ref, wgd_ref, gup_ref, gbias_ref)
    h = u[:, :D_CONV] * _sigmoid(u[:, D_CONV:2 * D_CONV])
    hmeta_ref[...] = h[pad:, :]
    is_meta = lax.broadcasted_iota(jnp.int32, (CHUNK, 1), 0) >= pad
    k = jnp.where(is_meta, u[:, 2 * D_CONV + D_QK:2 * D_CONV + 2 * D_QK], 0.0)
    v = jnp.where(is_meta, u[:, 2 * D_CONV + 2 * D_QK:2 * D_CONV + 2 * D_QK + D_V], 0.0)
    lg = jnp.where(is_meta, lg, 0.0)
    b = _chunk_cumsum(lg, CHUNK)
    kd = k * jnp.exp(b[CHUNK - 1:CHUNK, :] - b)
    smeta_ref[...] = _state_delta(kd, v)


def _layer_kernel(x_ref, hmeta_ref, smeta_ref, lng_ref, lnb_ref, wmain_ref, wgd_ref, gup_ref,
                  gbias_ref, cw_ref, cb_ref, clg_ref, clb_ref, gng_ref, wout_ref,
                  l1g_ref, l1b_ref, w1_ref, w2_ref, l2g_ref, l2b_ref,
                  o_ref, hbuf, sbuf, s1buf, ubuf, lgbuf, resbuf, *, tiles_per_seq, n_tiles):
    tm = TILE_M
    n_chunks = tm // CHUNK
    step = pl.program_id(0)

    @pl.when(step == 0)
    def _():
        s1buf[...] = jnp.zeros((tm, D_MODEL), F32)
        resbuf[...] = jnp.zeros((tm, D_MODEL), F32)
        ubuf[...] = jnp.zeros((tm, D_MAIN), F32)
        lgbuf[...] = jnp.zeros((tm, D_QK), F32)

    @pl.when(jnp.clip(step - 1, 0, n_tiles - 1) % tiles_per_seq == 0)
    def _():
        hbuf[0:HALO - N_META, :] = jnp.zeros((HALO - N_META, D_CONV), F32)
        hbuf[HALO - N_META:HALO, :] = hmeta_ref[...]
        sbuf[...] = smeta_ref[...]

    s1_prev = s1buf[...]
    s1b = s1_prev.astype(BF16)
    u = ubuf[...]
    lg = lgbuf[...]
    s = resbuf[...]
    q = u[:, 2 * D_CONV:2 * D_CONV + D_QK]
    k = u[:, 2 * D_CONV + D_QK:2 * D_CONV + 2 * D_QK]
    v = u[:, 2 * D_CONV + 2 * D_QK:2 * D_CONV + 2 * D_QK + D_V]
    r = u[:, 2 * D_CONV + 2 * D_QK + D_V:D_MAIN]
    n_up = D_FF // W_COLS

    def up_block(c):
        hb = jnp.maximum(_dot(s1b, w1_ref[c]), 0.0)
        return (hb * hb).astype(BF16)

    b = _chunk_cumsum(lg, tm)
    total_cols = _chunk_total_cols(lg, tm)
    hid2 = [up_block(0), up_block(1)]

    qe = q * (GLA_DK ** -0.5) * jnp.exp(b)
    ke = (k * jnp.exp(-b)).astype(BF16)
    vb = v.astype(BF16)
    ri = lax.broadcasted_iota(jnp.int32, (tm, tm), 0)
    ci = lax.broadcasted_iota(jnp.int32, (tm, tm), 1)
    causal = (ri // CHUNK == ci // CHUNK) & (ci <= ri)
    lane_head = lax.broadcasted_iota(jnp.int32, (1, D_QK), 1) // GLA_DK
    qe_heads = [jnp.where(lane_head == h, qe, 0.0).astype(BF16) for h in range(GLA_HEADS)]
    scores = [_dot_t1(qe_heads[h], ke) for h in range(GLA_HEADS)]
    deltas = []
    for c in range(n_chunks):
        rows = slice(c * CHUNK, (c + 1) * CHUNK)
        b_c = b[rows, :]
        kd = k[rows, :] * jnp.exp(b_c[CHUNK - 1:CHUNK, :] - b_c)
        deltas.append(_state_delta(kd, v[rows, :]))
    hid2 += [up_block(2), up_block(3)]

    o_heads = [_dot(jnp.where(causal, scores[h], 0.0).astype(BF16),
                    vb[:, h * GLA_DV:(h + 1) * GLA_DV]) for h in range(GLA_HEADS)]
    states = []
    state = sbuf[...]
    for c in range(n_chunks):
        states.append(state.astype(BF16))
        state = jnp.exp(total_cols[:, c * LANES:(c + 1) * LANES]) * state + deltas[c]
    hid2 += [up_block(4), up_block(5)]

    o_inter = [[_dot(qe_heads[h][c * CHUNK:(c + 1) * CHUNK, :], states[c]) for c in range(n_chunks)]
               for h in range(GLA_HEADS)]
    hid2 += [up_block(6), up_block(7)]

    hid2 = jnp.concatenate(hid2, axis=1)
    f = jnp.concatenate([_dot(hid2, w2_ref[c]) for c in range(D_MODEL // W_COLS)], axis=1)

    hbuf[HALO:HALO + tm, :] = u[:, :D_CONV] * _sigmoid(u[:, D_CONV:2 * D_CONV])
    conv = _causal_conv(hbuf, cw_ref, tm) + cb_ref[...]
    halo = hbuf[tm:tm + HALO, :]
    conv_out = _silu(_layer_norm(conv, clg_ref[...], clb_ref[...]))
    gla_blocks = []
    for h in range(GLA_HEADS):
        o = o_heads[h] + jnp.concatenate(o_inter[h], axis=0)
        o = o * lax.rsqrt(jnp.mean(o * o, axis=-1, keepdims=True) + LN_EPS) * gng_ref[...]
        gla_blocks.append(o * _silu(r[:, h * GLA_DV:(h + 1) * GLA_DV]))

    mixin = jnp.concatenate([conv_out] + gla_blocks, axis=-1).astype(BF16)
    mix = jnp.concatenate([_dot(mixin, wout_ref[c]) for c in range(D_MODEL // W_COLS)], axis=1)
    s1_new = _layer_norm(DEEPNORM_ALPHA * s + mix, l1g_ref[...], l1b_ref[...])

    out = _layer_norm(DEEPNORM_ALPHA * s1_prev + f, l2g_ref[...], l2b_ref[...])

    s_new = _layer_norm(x_ref[...], lng_ref[...], lnb_ref[...])
    u_new, lg_new = _in_projection(s_new, wmain_ref, wgd_ref, gup_ref, gbias_ref)
    hbuf[0:HALO, :] = halo
    sbuf[...] = state
    s1buf[...] = s1_new
    o_ref[...] = out
    resbuf[...] = s_new
    ubuf[...] = u_new
    lgbuf[...] = lg_new


def _resident(shape):
    return pl.BlockSpec(shape, lambda *_: (0,) * len(shape), pipeline_mode=pl.Buffered(1))


def kernel(x, meta_tokens, ln_in_g, ln_in_b, w_in, conv_w, conv_b, conv_ln_g, conv_ln_b,
           gate_up, gate_bias, gla_norm_g, w_out, ln1_g, ln1_b, w_ff1, w_ff2, ln2_g, ln2_b):
    batch, seq, d_model = x.shape
    assert d_model == D_MODEL and seq % TILE_M == 0 and w_in.shape[0] == DEPTH
    row = lambda a: a.reshape(1, -1).astype(F32)

    def col_blocks(w):
        kdim, ndim = w.shape
        return w.reshape(kdim, ndim // W_COLS, W_COLS).transpose(1, 0, 2).astype(BF16)

    w_main = w_in[0, :, :D_MAIN].astype(BF16)
    w_gd = jnp.pad(w_in[0, :, D_MAIN:], ((0, 0), (0, LANES - GLA_GATE_RANK))).astype(BF16)
    gup = jnp.pad(gate_up[0], ((0, LANES - GLA_GATE_RANK), (0, 0))).astype(BF16)
    lng, lnb, gbias = row(ln_in_g), row(ln_in_b), row(gate_bias[0])

    conv_taps = jnp.broadcast_to(conv_w[0].astype(F32)[:, None, :], (CONV_WIDTH, SUBLANES, D_CONV))

    proj_args = (lng, lnb, w_main, w_gd, gup, gbias)
    h_meta, s_meta = pl.pallas_call(
        _meta_kernel,
        out_shape=(jax.ShapeDtypeStruct((N_META, D_CONV), F32),
                   jax.ShapeDtypeStruct((D_QK, GLA_DV), F32)),
        compiler_params=pltpu.CompilerParams(vmem_limit_bytes=VMEM_LIMIT_BYTES),
        name="meta_state",
    )(meta_tokens.astype(F32), *proj_args)

    layer_args = (
        h_meta, s_meta, *proj_args,
        conv_taps, row(conv_b[0]), row(conv_ln_g[0]), row(conv_ln_b[0]),
        row(gla_norm_g[0]), col_blocks(w_out[0]), row(ln1_g[0]), row(ln1_b[0]),
        col_blocks(w_ff1[0]), col_blocks(w_ff2[0]), row(ln2_g[0]), row(ln2_b[0]),
    )
    tiles_per_seq = seq // TILE_M
    n_tiles = batch * tiles_per_seq

    def in_tile(i):
        j = jnp.minimum(i, n_tiles - 1)
        return (j // tiles_per_seq, j % tiles_per_seq, 0)

    def out_tile(i):
        j = jnp.maximum(i - PIPE_DEPTH, 0)
        return (j // tiles_per_seq, j % tiles_per_seq, 0)

    return pl.pallas_call(
        functools.partial(_layer_kernel, tiles_per_seq=tiles_per_seq, n_tiles=n_tiles),
        grid=(n_tiles + PIPE_DEPTH,),
        in_specs=[pl.BlockSpec((None, TILE_M, D_MODEL), in_tile)]
        + [_resident(a.shape) for a in layer_args],
        out_specs=pl.BlockSpec((None, TILE_M, D_MODEL), out_tile),
        out_shape=jax.ShapeDtypeStruct(x.shape, x.dtype),
        scratch_shapes=[pltpu.VMEM((HALO + TILE_M, D_CONV), F32),
                        pltpu.VMEM((D_QK, GLA_DV), F32),
                        pltpu.VMEM((TILE_M, D_MODEL), F32),
                        pltpu.VMEM((TILE_M, D_MAIN), F32),
                        pltpu.VMEM((TILE_M, D_QK), F32),
                        pltpu.VMEM((TILE_M, D_MODEL), F32)],
        compiler_params=pltpu.CompilerParams(
            dimension_semantics=("arbitrary",),
            vmem_limit_bytes=VMEM_LIMIT_BYTES),
        name="layer",
    )(x, *layer_args)
```

```python
import functools

import jax
import jax.numpy as jnp
from jax import lax
from jax.experimental import pallas as pl
from jax.experimental.pallas import tpu as pltpu

D_MODEL = 1024
N_META = 16
D_CONV = 512
CONV_WIDTH = 31
GLA_HEADS = 4
GLA_DV = 128
GLA_DK = 64
GLA_GATE_RANK = 16
GLA_TAU = 16.0
CHUNK = 64
D_FF = 4096
LN_EPS = 1e-5
DEPTH = 1
DEEPNORM_ALPHA = (2.0 * DEPTH) ** 0.25

D_QK = GLA_HEADS * GLA_DK
D_V = GLA_HEADS * GLA_DV
D_MAIN = 2 * D_CONV + 2 * D_QK + 2 * D_V
LANES = 128
SUBLANES = 8
HALO = 32
W_COLS = 512
PIPE_DEPTH = 2
TILE_M = 256
VMEM_LIMIT_BYTES = 56 * 1024 * 1024

F32 = jnp.float32
BF16 = jnp.bfloat16


def _dot(a, b):
    return jnp.dot(a, b, preferred_element_type=F32)


def _dot_t0(a, b):
    return lax.dot_general(a, b, (((0,), (0,)), ((), ())), preferred_element_type=F32)


def _dot_t1(a, b):
    return lax.dot_general(a, b, (((1,), (1,)), ((), ())), preferred_element_type=F32)


def _layer_norm(x, g, b):
    mu = jnp.mean(x, axis=-1, keepdims=True)
    xc = x - mu
    var = jnp.mean(xc * xc, axis=-1, keepdims=True)
    return xc * lax.rsqrt(var + LN_EPS) * g + b


def _sigmoid(x):
    return 1.0 / (1.0 + jnp.exp(-x))


def _silu(x):
    return x * _sigmoid(x)


def _log_sigmoid(z):
    return jnp.minimum(z, 0.0) - jnp.log(1.0 + jnp.exp(-jnp.abs(z)))


def _zero_after(x):
    bits = lax.bitcast_convert_type(x, jnp.uint32)
    half = jnp.uint32(16)
    zero_bits = lax.shift_right_logical(lax.shift_right_logical(bits, half), half)
    return lax.bitcast_convert_type(zero_bits, F32)


def _split_hi_lo(x):
    hi = x.astype(BF16)
    lo = (x - hi.astype(F32)).astype(BF16)
    return hi, lo


def _chunk_cumsum(lg, rows):
    ri = lax.broadcasted_iota(jnp.int32, (rows, rows), 0)
    ci = lax.broadcasted_iota(jnp.int32, (rows, rows), 1)
    tri = jnp.where((ri // CHUNK == ci // CHUNK) & (ci <= ri), 1.0, 0.0).astype(BF16)
    hi, lo = _split_hi_lo(lg)
    return _dot(tri, hi) + _dot(tri, lo)


def _chunk_total_cols(lg, rows):
    n_chunks = rows // CHUNK
    ri = lax.broadcasted_iota(jnp.int32, (rows, n_chunks * LANES), 0)
    ci = lax.broadcasted_iota(jnp.int32, (rows, n_chunks * LANES), 1)
    ind = jnp.where(ri // CHUNK == ci // LANES, 1.0, 0.0).astype(BF16)
    hi, lo = _split_hi_lo(lg)
    return _dot_t0(hi, ind) + _dot_t0(lo, ind)


def _state_delta(kd_c, v_c):
    full = _dot_t0(kd_c.astype(BF16), v_c.astype(BF16))
    return jnp.concatenate(
        [full[h * GLA_DK:(h + 1) * GLA_DK, h * GLA_DV:(h + 1) * GLA_DV] for h in range(GLA_HEADS)],
        axis=0)


def _causal_conv(hbuf, cw_ref, n_rows):
    s0 = HALO - (CONV_WIDTH - 1)
    n_groups = n_rows // SUBLANES
    sub = lax.broadcasted_iota(jnp.int32, (SUBLANES, LANES), 0)
    n_base = (CONV_WIDTH + s0 - 2) // SUBLANES + 1

    def taps(phase):
        return [(a, SUBLANES * a + phase - s0) for a in range(n_base + 1)
                if 0 <= SUBLANES * a + phase - s0 < CONV_WIDTH]

    def piece(p, lanes):
        return hbuf[p * SUBLANES:(p + 1) * SUBLANES, lanes]

    def phase_sum(phase, pieces, first):
        acc = None
        for a, tap in taps(phase):
            term = pieces[a - first] * cw_ref[tap, :, lanes_of[0]]
            acc = term if acc is None else acc + term
        return acc

    strips = []
    chain = None
    lanes_of = [None]
    for lt in range(D_CONV // LANES):
        lanes = slice(lt * LANES, (lt + 1) * LANES)
        lanes_of[0] = lanes
        first_pieces = [piece(a, lanes) for a in range(n_base)]
        rolled = {r: pltpu.roll(phase_sum(r, first_pieces, 0), SUBLANES - r, 0)
                  for r in range(1, SUBLANES)}
        outs = []
        for i in range(n_groups):
            pieces = [piece(i + 1 + a, lanes) for a in range(n_base)]
            if chain is not None:
                pieces[-1] = pieces[-1] + chain
            out = phase_sum(0, pieces, 1)
            for r in range(1, SUBLANES):
                nxt = pltpu.roll(phase_sum(r, pieces, 0), SUBLANES - r, 0)
                out = out + jnp.where(sub < SUBLANES - r, rolled[r], nxt)
                rolled[r] = nxt
            outs.append(out)
            chain = _zero_after(out)
        strips.append(jnp.concatenate(outs, axis=0))
    return jnp.concatenate(strips, axis=1)


def _in_projection(s, wmain_ref, wgd_ref, gup_ref, gbias_ref):
    sb = s.astype(BF16)
    u = _dot(sb, wmain_ref[...])
    gd = _dot(sb, wgd_ref[...])
    z = _dot(gd.astype(BF16), gup_ref[...]) + gbias_ref[...]
    lg = _log_sigmoid(z) * (1.0 / GLA_TAU)
    return u, lg


def _meta_kernel(meta_ref, lng_ref, lnb_ref, wmain_ref, wgd_ref, gup_ref, gbias_ref,
                 hmeta_ref, smeta_ref):
    pad = CHUNK - N_META
    s = _layer_norm(meta_ref[...], lng_ref[...], lnb_ref[...])
    s = jnp.concatenate([jnp.zeros((pad, D_MODEL), F32), s], axis=0)
    u, lg = _in_projection(s, wmain_ref, wgd_ref, gup_ref, gbias_ref)
    h = u[:, :D_CONV] * _sigmoid(u[:, D_CONV:2 * D_CONV])
    hmeta_ref[...] = h[pad:, :]
    is_meta = lax.broadcasted_iota(jnp.int32, (CHUNK, 1), 0) >= pad
    k = jnp.where(is_meta, u[:, 2 * D_CONV + D_QK:2 * D_CONV + 2 * D_QK], 0.0)
    v = jnp.where(is_meta, u[:, 2 * D_CONV + 2 * D_QK:2 * D_CONV + 2 * D_QK + D_V], 0.0)
    lg = jnp.where(is_meta, lg, 0.0)
    b = _chunk_cumsum(lg, CHUNK)
    kd = k * jnp.exp(b[CHUNK - 1:CHUNK, :] - b)
    smeta_ref[...] = _state_delta(kd, v)


def _layer_kernel(x_ref, hmeta_ref, smeta_ref, lng_ref, lnb_ref, wmain_ref, wgd_ref, gup_ref,
                  gbias_ref, cw_ref, cb_ref, clg_ref, clb_ref, gng_ref, l1g_ref, l1b_ref,
                  l2g_ref, l2b_ref, *rest, tiles_per_seq, n_tiles):
    n_out, n_up = D_MODEL // W_COLS, D_FF // W_COLS
    wout_refs, w1_refs, w2_refs = rest[:n_out], rest[n_out:n_out + n_up], rest[n_out + n_up:2 * n_out + n_up]
    o_ref, hbuf, sbuf, s1buf, ubuf, lgbuf, resbuf = rest[2 * n_out + n_up:]
    tm = TILE_M
    n_chunks = tm // CHUNK
    step = pl.program_id(0)

    @pl.when(step == 0)
    def _():
        s1buf[...] = jnp.zeros((tm, D_MODEL), F32)
        resbuf[...] = jnp.zeros((tm, D_MODEL), F32)
        ubuf[...] = jnp.zeros((tm, D_MAIN), F32)
        lgbuf[...] = jnp.zeros((tm, D_QK), F32)

    @pl.when(jnp.clip(step - 1, 0, n_tiles - 1) % tiles_per_seq == 0)
    def _():
        hbuf[0:HALO - N_META, :] = jnp.zeros((HALO - N_META, D_CONV), F32)
        hbuf[HALO - N_META:HALO, :] = hmeta_ref[...]
        sbuf[...] = smeta_ref[...]

    s1_prev = s1buf[...]
    s1b = s1_prev.astype(BF16)
    u = ubuf[...]
    lg = lgbuf[...]
    s = resbuf[...]
    q = u[:, 2 * D_CONV:2 * D_CONV + D_QK]
    k = u[:, 2 * D_CONV + D_QK:2 * D_CONV + 2 * D_QK]
    v = u[:, 2 * D_CONV + 2 * D_QK:2 * D_CONV + 2 * D_QK + D_V]
    r = u[:, 2 * D_CONV + 2 * D_QK + D_V:D_MAIN]

    def up_block(c):
        hb = jnp.maximum(_dot(s1b, w1_refs[c][...]), 0.0)
        return (hb * hb).astype(BF16)

    b = _chunk_cumsum(lg, tm)
    total_cols = _chunk_total_cols(lg, tm)
    hid2 = [up_block(0), up_block(1)]

    qe = q * (GLA_DK ** -0.5) * jnp.exp(b)
    ke = (k * jnp.exp(-b)).astype(BF16)
    vb = v.astype(BF16)
    ri = lax.broadcasted_iota(jnp.int32, (tm, tm), 0)
    ci = lax.broadcasted_iota(jnp.int32, (tm, tm), 1)
    causal = (ri // CHUNK == ci // CHUNK) & (ci <= ri)
    lane_head = lax.broadcasted_iota(jnp.int32, (1, D_QK), 1) // GLA_DK
    qe_heads = [jnp.where(lane_head == h, qe, 0.0).astype(BF16) for h in range(GLA_HEADS)]
    scores = [_dot_t1(qe_heads[h], ke) for h in range(GLA_HEADS)]
    deltas = []
    for c in range(n_chunks):
        rows = slice(c * CHUNK, (c + 1) * CHUNK)
        b_c = b[rows, :]
        kd = k[rows, :] * jnp.exp(b_c[CHUNK - 1:CHUNK, :] - b_c)
        deltas.append(_state_delta(kd, v[rows, :]))
    hid2 += [up_block(2), up_block(3)]

    o_heads = [_dot(jnp.where(causal, scores[h], 0.0).astype(BF16),
                    vb[:, h * GLA_DV:(h + 1) * GLA_DV]) for h in range(GLA_HEADS)]
    states = []
    state = sbuf[...]
    for c in range(n_chunks):
        states.append(state.astype(BF16))
        state = jnp.exp(total_cols[:, c * LANES:(c + 1) * LANES]) * state + deltas[c]
    hid2 += [up_block(4), up_block(5)]

    o_inter = [[_dot(qe_heads[h][c * CHUNK:(c + 1) * CHUNK, :], states[c]) for c in range(n_chunks)]
               for h in range(GLA_HEADS)]
    hid2 += [up_block(6), up_block(7)]

    hid2 = jnp.concatenate(hid2, axis=1)
    f = jnp.concatenate([_dot(hid2, w[...]) for w in w2_refs], axis=1)

    hbuf[HALO:HALO + tm, :] = u[:, :D_CONV] * _sigmoid(u[:, D_CONV:2 * D_CONV])
    conv = _causal_conv(hbuf, cw_ref, tm) + cb_ref[...]
    halo = hbuf[tm:tm + HALO, :]
    conv_out = _silu(_layer_norm(conv, clg_ref[...], clb_ref[...]))
    gla_blocks = []
    for h in range(GLA_HEADS):
        o = o_heads[h] + jnp.concatenate(o_inter[h], axis=0)
        o = o * lax.rsqrt(jnp.mean(o * o, axis=-1, keepdims=True) + LN_EPS) * gng_ref[...]
        gla_blocks.append(o * _silu(r[:, h * GLA_DV:(h + 1) * GLA_DV]))

    mixin = jnp.concatenate([conv_out] + gla_blocks, axis=-1).astype(BF16)
    mix = jnp.concatenate([_dot(mixin, w[...]) for w in wout_refs], axis=1)
    s1_new = _layer_norm(DEEPNORM_ALPHA * s + mix, l1g_ref[...], l1b_ref[...])

    out = _layer_norm(DEEPNORM_ALPHA * s1_prev + f, l2g_ref[...], l2b_ref[...])

    s_new = _layer_norm(x_ref[...], lng_ref[...], lnb_ref[...])
    u_new, lg_new = _in_projection(s_new, wmain_ref, wgd_ref, gup_ref, gbias_ref)
    hbuf[0:HALO, :] = halo
    sbuf[...] = state
    s1buf[...] = s1_new
    o_ref[...] = out
    resbuf[...] = s_new
    ubuf[...] = u_new
    lgbuf[...] = lg_new


def _resident(shape):
    return pl.BlockSpec(shape, lambda *_: (0,) * len(shape), pipeline_mode=pl.Buffered(1))


def _resident_col_blocks(w):
    kdim, ndim = w.shape
    return [pl.BlockSpec((kdim, W_COLS), lambda *_, c=c: (0, c), pipeline_mode=pl.Buffered(1))
            for c in range(ndim // W_COLS)]


def kernel(x, meta_tokens, ln_in_g, ln_in_b, w_in, conv_w, conv_b, conv_ln_g, conv_ln_b,
           gate_up, gate_bias, gla_norm_g, w_out, ln1_g, ln1_b, w_ff1, w_ff2, ln2_g, ln2_b):
    batch, seq, d_model = x.shape
    assert d_model == D_MODEL and seq % TILE_M == 0 and w_in.shape[0] == DEPTH
    row = lambda a: a.reshape(1, -1).astype(F32)

    w_main = w_in[0, :, :D_MAIN].astype(BF16)
    w_gd = jnp.pad(w_in[0, :, D_MAIN:], ((0, 0), (0, LANES - GLA_GATE_RANK))).astype(BF16)
    gup = jnp.pad(gate_up[0], ((0, LANES - GLA_GATE_RANK), (0, 0))).astype(BF16)
    lng, lnb, gbias = row(ln_in_g), row(ln_in_b), row(gate_bias[0])

    conv_taps = jnp.broadcast_to(conv_w[0].astype(F32)[:, None, :], (CONV_WIDTH, SUBLANES, D_CONV))

    proj_args = (lng, lnb, w_main, w_gd, gup, gbias)
    h_meta, s_meta = pl.pallas_call(
        _meta_kernel,
        out_shape=(jax.ShapeDtypeStruct((N_META, D_CONV), F32),
                   jax.ShapeDtypeStruct((D_QK, GLA_DV), F32)),
        compiler_params=pltpu.CompilerParams(vmem_limit_bytes=VMEM_LIMIT_BYTES),
        name="meta_state",
    )(meta_tokens.astype(F32), *proj_args)

    layer_args = (
        h_meta, s_meta, *proj_args,
        conv_taps, row(conv_b[0]), row(conv_ln_g[0]), row(conv_ln_b[0]),
        row(gla_norm_g[0]), row(ln1_g[0]), row(ln1_b[0]), row(ln2_g[0]), row(ln2_b[0]),
    )
    blocked = (w_out[0].astype(BF16), w_ff1[0].astype(BF16), w_ff2[0].astype(BF16))
    blocked_specs = [spec for w in blocked for spec in _resident_col_blocks(w)]
    blocked_args = [w for w in blocked for _ in range(w.shape[1] // W_COLS)]
    tiles_per_seq = seq // TILE_M
    n_tiles = batch * tiles_per_seq

    def in_tile(i):
        j = jnp.minimum(i, n_tiles - 1)
        return (j // tiles_per_seq, j % tiles_per_seq, 0)

    def out_tile(i):
        j = jnp.maximum(i - PIPE_DEPTH, 0)
        return (j // tiles_per_seq, j % tiles_per_seq, 0)

    return pl.pallas_call(
        functools.partial(_layer_kernel, tiles_per_seq=tiles_per_seq, n_tiles=n_tiles),
        grid=(n_tiles + PIPE_DEPTH,),
        in_specs=[pl.BlockSpec((None, TILE_M, D_MODEL), in_tile)]
        + [_resident(a.shape) for a in layer_args] + blocked_specs,
        out_specs=pl.BlockSpec((None, TILE_M, D_MODEL), out_tile),
        out_shape=jax.ShapeDtypeStruct(x.shape, x.dtype),
        scratch_shapes=[pltpu.VMEM((HALO + TILE_M, D_CONV), F32),
                        pltpu.VMEM((D_QK, GLA_DV), F32),
                        pltpu.VMEM((TILE_M, D_MODEL), F32),
                        pltpu.VMEM((TILE_M, D_MAIN), F32),
                        pltpu.VMEM((TILE_M, D_QK), F32),
                        pltpu.VMEM((TILE_M, D_MODEL), F32)],
        compiler_params=pltpu.CompilerParams(
            dimension_semantics=("arbitrary",),
            vmem_limit_bytes=VMEM_LIMIT_BYTES),
        name="layer",
    )(x, *layer_args, *blocked_args)
```

```python
import functools

import jax
import jax.numpy as jnp
from jax import lax
from jax.experimental import pallas as pl
from jax.experimental.pallas import tpu as pltpu

D_MODEL = 1024
N_META = 16
D_CONV = 512
CONV_WIDTH = 31
GLA_HEADS = 4
GLA_DV = 128
GLA_DK = 64
GLA_GATE_RANK = 16
GLA_TAU = 16.0
CHUNK = 64
D_FF = 4096
LN_EPS = 1e-5
DEPTH = 1
DEEPNORM_ALPHA = (2.0 * DEPTH) ** 0.25

D_QK = GLA_HEADS * GLA_DK
D_V = GLA_HEADS * GLA_DV
D_MAIN = 2 * D_CONV + 2 * D_QK + 2 * D_V
LANES = 128
SUBLANES = 8
HALO = 32
CONV_HOP_EVERY = 3
W_COLS = 512
PIPE_DEPTH = 2
TILE_M = 256
VMEM_LIMIT_BYTES = 56 * 1024 * 1024

F32 = jnp.float32
BF16 = jnp.bfloat16


def _dot(a, b):
    return jnp.dot(a, b, preferred_element_type=F32)


def _dot_t0(a, b):
    return lax.dot_general(a, b, (((0,), (0,)), ((), ())), preferred_element_type=F32)


def _dot_t1(a, b):
    return lax.dot_general(a, b, (((1,), (1,)), ((), ())), preferred_element_type=F32)


def _layer_norm(x, g, b):
    mu = jnp.mean(x, axis=-1, keepdims=True)
    xc = x - mu
    var = jnp.mean(xc * xc, axis=-1, keepdims=True)
    return xc * lax.rsqrt(var + LN_EPS) * g + b


def _sigmoid(x):
    return 1.0 / (1.0 + jnp.exp(-x))


def _silu(x):
    return x * _sigmoid(x)


def _log_sigmoid(z):
    return jnp.minimum(z, 0.0) - jnp.log(1.0 + jnp.exp(-jnp.abs(z)))


def _zero_after(x):
    bits = lax.bitcast_convert_type(x, jnp.uint32)
    half = jnp.uint32(16)
    zero_bits = lax.shift_right_logical(lax.shift_right_logical(bits, half), half)
    return lax.bitcast_convert_type(zero_bits, F32)


def _split_hi_lo(x):
    hi = x.astype(BF16)
    lo = (x - hi.astype(F32)).astype(BF16)
    return hi, lo


def _chunk_cumsum(lg, rows):
    ri = lax.broadcasted_iota(jnp.int32, (rows, rows), 0)
    ci = lax.broadcasted_iota(jnp.int32, (rows, rows), 1)
    tri = jnp.where((ri // CHUNK == ci // CHUNK) & (ci <= ri), 1.0, 0.0).astype(BF16)
    hi, lo = _split_hi_lo(lg)
    return _dot(tri, hi) + _dot(tri, lo)


def _chunk_total_cols(lg, rows):
    n_chunks = rows // CHUNK
    ri = lax.broadcasted_iota(jnp.int32, (rows, n_chunks * LANES), 0)
    ci = lax.broadcasted_iota(jnp.int32, (rows, n_chunks * LANES), 1)
    ind = jnp.where(ri // CHUNK == ci // LANES, 1.0, 0.0).astype(BF16)
    hi, lo = _split_hi_lo(lg)
    return _dot_t0(hi, ind) + _dot_t0(lo, ind)


def _state_delta(kd_c, v_c):
    full = _dot_t0(kd_c.astype(BF16), v_c.astype(BF16))
    return jnp.concatenate(
        [full[h * GLA_DK:(h + 1) * GLA_DK, h * GLA_DV:(h + 1) * GLA_DV] for h in range(GLA_HEADS)],
        axis=0)


def _causal_conv(hbuf, cw_ref, n_rows):
    s0 = HALO - (CONV_WIDTH - 1)
    n_groups = n_rows // SUBLANES
    sub = lax.broadcasted_iota(jnp.int32, (SUBLANES, LANES), 0)
    n_base = (CONV_WIDTH + s0 - 2) // SUBLANES + 1

    def taps(phase):
        return [(a, SUBLANES * a + phase - s0) for a in range(n_base + 1)
                if 0 <= SUBLANES * a + phase - s0 < CONV_WIDTH]

    def piece(p, lanes):
        return hbuf[p * SUBLANES:(p + 1) * SUBLANES, lanes]

    def phase_sum(phase, pieces, first):
        acc = None
        for a, tap in taps(phase):
            term = pieces[a - first] * cw_ref[tap, :, lanes_of[0]]
            acc = term if acc is None else acc + term
        return acc

    strips = []
    chain = None
    lanes_of = [None]
    for lt in range(D_CONV // LANES):
        lanes = slice(lt * LANES, (lt + 1) * LANES)
        lanes_of[0] = lanes
        first_pieces = [piece(a, lanes) for a in range(n_base)]
        rolled = {r: pltpu.roll(phase_sum(r, first_pieces, 0), SUBLANES - r, 0)
                  for r in range(1, SUBLANES)}
        outs = []
        for i in range(n_groups):
            pieces = [piece(i + 1 + a, lanes) for a in range(n_base)]
            if chain is not None:
                pieces[-1] = pieces[-1] + chain
            out = phase_sum(0, pieces, 1)
            for r in range(1, SUBLANES):
                nxt = pltpu.roll(phase_sum(r, pieces, 0), SUBLANES - r, 0)
                out = out + jnp.where(sub < SUBLANES - r, rolled[r], nxt)
                rolled[r] = nxt
            outs.append(out)
            chain = _zero_after(out)
            if (lt * n_groups + i) % CONV_HOP_EVERY == 0:
                chain = pltpu.roll(chain, 1, 1)
        strips.append(jnp.concatenate(outs, axis=0))
    return jnp.concatenate(strips, axis=1)


def _gate_projection(sb, wgd_ref, gup_ref, gbias_ref):
    gd = _dot(sb, wgd_ref[...])
    z = _dot(gd.astype(BF16), gup_ref[...]) + gbias_ref[...]
    return _log_sigmoid(z) * (1.0 / GLA_TAU)


def _in_projection(s, wmain_ref, wgd_ref, gup_ref, gbias_ref):
    sb = s.astype(BF16)
    u = _dot(sb, wmain_ref[...])
    return u, _gate_projection(sb, wgd_ref, gup_ref, gbias_ref)


def _meta_kernel(meta_ref, lng_ref, lnb_ref, wmain_ref, wgd_ref, gup_ref, gbias_ref,
                 hmeta_ref, smeta_ref):
    pad = CHUNK - N_META
    s = _layer_norm(meta_ref[...], lng_ref[...], lnb_ref[...])
    s = jnp.concatenate([jnp.zeros((pad, D_MODEL), F32), s], axis=0)
    u, lg = _in_projection(s, wmain_ref, wgd_ref, gup_ref, gbias_ref)
    h = u[:, :D_CONV] * _sigmoid(u[:, D_CONV:2 * D_CONV])
    hmeta_ref[...] = h[pad:, :]
    is_meta = lax.broadcasted_iota(jnp.int32, (CHUNK, 1), 0) >= pad
    k = jnp.where(is_meta, u[:, 2 * D_CONV + D_QK:2 * D_CONV + 2 * D_QK], 0.0)
    v = jnp.where(is_meta, u[:, 2 * D_CONV + 2 * D_QK:2 * D_CONV + 2 * D_QK + D_V], 0.0)
    lg = jnp.where(is_meta, lg, 0.0)
    b = _chunk_cumsum(lg, CHUNK)
    kd = k * jnp.exp(b[CHUNK - 1:CHUNK, :] - b)
    smeta_ref[...] = _state_delta(kd, v)


def _layer_kernel(x_ref, hmeta_ref, smeta_ref, lng_ref, lnb_ref, wmain_ref, wgd_ref, gup_ref,
                  gbias_ref, cw_ref, cb_ref, clg_ref, clb_ref, gng_ref, l1g_ref, l1b_ref,
                  l2g_ref, l2b_ref, *rest, tiles_per_seq, n_tiles):
    n_out, n_up = D_MODEL // W_COLS, D_FF // W_COLS
    wout_refs, w1_refs, w2_refs = rest[:n_out], rest[n_out:n_out + n_up], rest[n_out + n_up:2 * n_out + n_up]
    o_ref, hbuf, sbuf, s1buf, ubuf, lgbuf, resbuf = rest[2 * n_out + n_up:]
    tm = TILE_M
    n_chunks = tm // CHUNK
    step = pl.program_id(0)

    @pl.when(step == 0)
    def _():
        s1buf[...] = jnp.zeros((tm, D_MODEL), F32)
        resbuf[...] = jnp.zeros((tm, D_MODEL), F32)
        ubuf[...] = jnp.zeros((tm, D_MAIN), F32)
        lgbuf[...] = jnp.zeros((tm, D_QK), F32)

    @pl.when(jnp.clip(step - 1, 0, n_tiles - 1) % tiles_per_seq == 0)
    def _():
        hbuf[0:HALO - N_META, :] = jnp.zeros((HALO - N_META, D_CONV), F32)
        hbuf[HALO - N_META:HALO, :] = hmeta_ref[...]
        sbuf[...] = smeta_ref[...]

    s1_prev = s1buf[...]
    s1b = s1_prev.astype(BF16)
    u = ubuf[...]
    lg = lgbuf[...]
    s = resbuf[...]
    q = u[:, 2 * D_CONV:2 * D_CONV + D_QK]
    k = u[:, 2 * D_CONV + D_QK:2 * D_CONV + 2 * D_QK]
    v = u[:, 2 * D_CONV + 2 * D_QK:2 * D_CONV + 2 * D_QK + D_V]
    r = u[:, 2 * D_CONV + 2 * D_QK + D_V:D_MAIN]

    def up_block(c):
        hb = jnp.maximum(_dot(s1b, w1_refs[c][...]), 0.0)
        return (hb * hb).astype(BF16)

    b = _chunk_cumsum(lg, tm)
    total_cols = _chunk_total_cols(lg, tm)
    hid2 = [up_block(0), up_block(1)]

    s_new = _layer_norm(x_ref[...], lng_ref[...], lnb_ref[...])
    sb_new = s_new.astype(BF16)
    lg_new = _gate_projection(sb_new, wgd_ref, gup_ref, gbias_ref)

    qe = q * (GLA_DK ** -0.5) * jnp.exp(b)
    ke = (k * jnp.exp(-b)).astype(BF16)
    vb = v.astype(BF16)
    ri = lax.broadcasted_iota(jnp.int32, (tm, tm), 0)
    ci = lax.broadcasted_iota(jnp.int32, (tm, tm), 1)
    causal = (ri // CHUNK == ci // CHUNK) & (ci <= ri)
    lane_head = lax.broadcasted_iota(jnp.int32, (1, D_QK), 1) // GLA_DK
    qe_heads = [jnp.where(lane_head == h, qe, 0.0).astype(BF16) for h in range(GLA_HEADS)]
    scores = [_dot_t1(qe_heads[h], ke) for h in range(GLA_HEADS)]
    deltas = []
    for c in range(n_chunks):
        rows = slice(c * CHUNK, (c + 1) * CHUNK)
        b_c = b[rows, :]
        kd = k[rows, :] * jnp.exp(b_c[CHUNK - 1:CHUNK, :] - b_c)
        deltas.append(_state_delta(kd, v[rows, :]))
    hid2 += [up_block(2), up_block(3)]

    zero_v = jnp.zeros((tm, GLA_DV), BF16)
    o_intra = []
    for h in range(0, GLA_HEADS, 2):
        a_pair = jnp.concatenate(
            [jnp.where(causal, scores[h + i], 0.0).astype(BF16) for i in range(2)], axis=1)
        v0, v1 = (vb[:, (h + i) * GLA_DV:(h + i + 1) * GLA_DV] for i in range(2))
        v_pair = jnp.concatenate([jnp.concatenate([v0, zero_v], axis=1),
                                  jnp.concatenate([zero_v, v1], axis=1)], axis=0)
        o_intra.append(_dot(a_pair, v_pair))
    states = []
    state = sbuf[...]
    for c in range(n_chunks):
        states.append(state.astype(BF16))
        state = jnp.exp(total_cols[:, c * LANES:(c + 1) * LANES]) * state + deltas[c]
    hid2 += [up_block(4), up_block(5)]

    qe_b = qe.astype(BF16)
    zero_s = jnp.zeros((GLA_DK, GLA_DV), BF16)
    o_inter = []
    for c in range(n_chunks):
        s_bd = jnp.concatenate(
            [jnp.concatenate([states[c][h * GLA_DK:(h + 1) * GLA_DK, :] if g == h else zero_s
                              for g in range(GLA_HEADS)], axis=1) for h in range(GLA_HEADS)], axis=0)
        o_inter.append(_dot(qe_b[c * CHUNK:(c + 1) * CHUNK, :], s_bd))
    hid2 += [up_block(6), up_block(7)]

    hid2 = jnp.concatenate(hid2, axis=1)
    f = jnp.concatenate([_dot(hid2, w[...]) for w in w2_refs], axis=1)

    hbuf[HALO:HALO + tm, :] = u[:, :D_CONV] * _sigmoid(u[:, D_CONV:2 * D_CONV])
    conv = _causal_conv(hbuf, cw_ref, tm) + cb_ref[...]
    halo = hbuf[tm:tm + HALO, :]
    conv_out = _silu(_layer_norm(conv, clg_ref[...], clb_ref[...]))
    o_all = jnp.concatenate(o_intra, axis=1) + jnp.concatenate(o_inter, axis=0)
    gla_blocks = []
    for h in range(GLA_HEADS):
        o = o_all[:, h * GLA_DV:(h + 1) * GLA_DV]
        o = o * lax.rsqrt(jnp.mean(o * o, axis=-1, keepdims=True) + LN_EPS) * gng_ref[...]
        gla_blocks.append(o * _silu(r[:, h * GLA_DV:(h + 1) * GLA_DV]))

    mixin = jnp.concatenate([conv_out] + gla_blocks, axis=-1).astype(BF16)
    mix = jnp.concatenate([_dot(mixin, w[...]) for w in wout_refs], axis=1)
    s1_new = _layer_norm(DEEPNORM_ALPHA * s + mix, l1g_ref[...], l1b_ref[...])

    out = _layer_norm(DEEPNORM_ALPHA * s1_prev + f, l2g_ref[...], l2b_ref[...])

    u_new = _dot(sb_new, wmain_ref[...])
    hbuf[0:HALO, :] = halo
    sbuf[...] = state
    s1buf[...] = s1_new
    o_ref[...] = out
    resbuf[...] = s_new
    ubuf[...] = u_new
    lgbuf[...] = lg_new


def _resident(shape):
    return pl.BlockSpec(shape, lambda *_: (0,) * len(shape), pipeline_mode=pl.Buffered(1))


def _resident_col_blocks(w):
    kdim, ndim = w.shape
    return [pl.BlockSpec((kdim, W_COLS), lambda *_, c=c: (0, c), pipeline_mode=pl.Buffered(1))
            for c in range(ndim // W_COLS)]


def kernel(x, meta_tokens, ln_in_g, ln_in_b, w_in, conv_w, conv_b, conv_ln_g, conv_ln_b,
           gate_up, gate_bias, gla_norm_g, w_out, ln1_g, ln1_b, w_ff1, w_ff2, ln2_g, ln2_b):
    batch, seq, d_model = x.shape
    assert d_model == D_MODEL and seq % TILE_M == 0 and w_in.shape[0] == DEPTH
    row = lambda a: a.reshape(1, -1).astype(F32)

    w_main = w_in[0, :, :D_MAIN].astype(BF16)
    w_gd = jnp.pad(w_in[0, :, D_MAIN:], ((0, 0), (0, LANES - GLA_GATE_RANK))).astype(BF16)
    gup = jnp.pad(gate_up[0], ((0, LANES - GLA_GATE_RANK), (0, 0))).astype(BF16)
    lng, lnb, gbias = row(ln_in_g), row(ln_in_b), row(gate_bias[0])

    conv_taps = jnp.broadcast_to(conv_w[0].astype(F32)[:, None, :], (CONV_WIDTH, SUBLANES, D_CONV))

    proj_args = (lng, lnb, w_main, w_gd, gup, gbias)
    h_meta, s_meta = pl.pallas_call(
        _meta_kernel,
        out_shape=(jax.ShapeDtypeStruct((N_META, D_CONV), F32),
                   jax.ShapeDtypeStruct((D_QK, GLA_DV), F32)),
        compiler_params=pltpu.CompilerParams(vmem_limit_bytes=VMEM_LIMIT_BYTES),
        name="meta_state",
    )(meta_tokens.astype(F32), *proj_args)

    layer_args = (
        h_meta, s_meta, *proj_args,
        conv_taps, row(conv_b[0]), row(conv_ln_g[0]), row(conv_ln_b[0]),
        row(gla_norm_g[0]), row(ln1_g[0]), row(ln1_b[0]), row(ln2_g[0]), row(ln2_b[0]),
    )
    blocked = (w_out[0].astype(BF16), w_ff1[0].astype(BF16), w_ff2[0].astype(BF16))
    blocked_specs = [spec for w in blocked for spec in _resident_col_blocks(w)]
    blocked_args = [w for w in blocked for _ in range(w.shape[1] // W_COLS)]
    tiles_per_seq = seq // TILE_M
    n_tiles = batch * tiles_per_seq

    def in_tile(i):
        j = jnp.minimum(i, n_tiles - 1)
        return (j // tiles_per_seq, j % tiles_per_seq, 0)

    def out_tile(i):
        j = jnp.maximum(i - PIPE_DEPTH, 0)
        return (j // tiles_per_seq, j % tiles_per_seq, 0)

    return pl.pallas_call(
        functools.partial(_layer_kernel, tiles_per_seq=tiles_per_seq, n_tiles=n_tiles),
        grid=(n_tiles + PIPE_DEPTH,),
        in_specs=[pl.BlockSpec((None, TILE_M, D_MODEL), in_tile)]
        + [_resident(a.shape) for a in layer_args] + blocked_specs,
        out_specs=pl.BlockSpec((None, TILE_M, D_MODEL), out_tile),
        out_shape=jax.ShapeDtypeStruct(x.shape, x.dtype),
        scratch_shapes=[pltpu.VMEM((HALO + TILE_M, D_CONV), F32),
                        pltpu.VMEM((D_QK, GLA_DV), F32),
                        pltpu.VMEM((TILE_M, D_MODEL), F32),
                        pltpu.VMEM((TILE_M, D_MAIN), F32),
                        pltpu.VMEM((TILE_M, D_QK), F32),
                        pltpu.VMEM((TILE_M, D_MODEL), F32)],
        compiler_params=pltpu.CompilerParams(
            dimension_semantics=("arbitrary",),
            vmem_limit_bytes=VMEM_LIMIT_BYTES),
        name="layer",
    )(x, *layer_args, *blocked_args)
```

```python
import functools

import jax
import jax.numpy as jnp
from jax import lax
from jax.experimental import pallas as pl
from jax.experimental.pallas import tpu as pltpu

D_MODEL = 1024
N_META = 16
D_CONV = 512
CONV_WIDTH = 31
GLA_HEADS = 4
GLA_DV = 128
GLA_DK = 64
GLA_GATE_RANK = 16
GLA_TAU = 16.0
CHUNK = 64
D_FF = 4096
LN_EPS = 1e-5
DEPTH = 1
DEEPNORM_ALPHA = (2.0 * DEPTH) ** 0.25

D_QK = GLA_HEADS * GLA_DK
D_V = GLA_HEADS * GLA_DV
D_MAIN = 2 * D_CONV + 2 * D_QK + 2 * D_V
LANES = 128
SUBLANES = 8
HALO = 32
CONV_HOP_EVERY = 3
W_COLS = 512
PIPE_DEPTH = 2
TILE_M = 256
VMEM_LIMIT_BYTES = 56 * 1024 * 1024

F32 = jnp.float32
BF16 = jnp.bfloat16


def _dot(a, b):
    return jnp.dot(a, b, preferred_element_type=F32)


def _dot_t0(a, b):
    return lax.dot_general(a, b, (((0,), (0,)), ((), ())), preferred_element_type=F32)


def _dot_t1(a, b):
    return lax.dot_general(a, b, (((1,), (1,)), ((), ())), preferred_element_type=F32)


def _layer_norm(x, g, b):
    mu = jnp.mean(x, axis=-1, keepdims=True)
    xc = x - mu
    var = jnp.mean(xc * xc, axis=-1, keepdims=True)
    return xc * lax.rsqrt(var + LN_EPS) * g + b


def _sigmoid(x):
    return 1.0 / (1.0 + jnp.exp(-x))


def _silu(x):
    return x * _sigmoid(x)


def _log_sigmoid(z):
    return jnp.minimum(z, 0.0) - jnp.log(1.0 + jnp.exp(-jnp.abs(z)))


def _zero_after(x):
    bits = lax.bitcast_convert_type(x, jnp.uint32)
    half = jnp.uint32(16)
    zero_bits = lax.shift_right_logical(lax.shift_right_logical(bits, half), half)
    return lax.bitcast_convert_type(zero_bits, F32)


def _split_hi_lo(x):
    hi = x.astype(BF16)
    lo = (x - hi.astype(F32)).astype(BF16)
    return hi, lo


def _chunk_cumsum(lg, rows):
    ri = lax.broadcasted_iota(jnp.int32, (rows, rows), 0)
    ci = lax.broadcasted_iota(jnp.int32, (rows, rows), 1)
    tri = jnp.where((ri // CHUNK == ci // CHUNK) & (ci <= ri), 1.0, 0.0).astype(BF16)
    hi, lo = _split_hi_lo(lg)
    return _dot(tri, hi) + _dot(tri, lo)


def _chunk_total_cols(lg, rows):
    n_chunks = rows // CHUNK
    ri = lax.broadcasted_iota(jnp.int32, (rows, n_chunks * LANES), 0)
    ci = lax.broadcasted_iota(jnp.int32, (rows, n_chunks * LANES), 1)
    ind = jnp.where(ri // CHUNK == ci // LANES, 1.0, 0.0).astype(BF16)
    hi, lo = _split_hi_lo(lg)
    return _dot_t0(hi, ind) + _dot_t0(lo, ind)


def _state_delta(kd_c, v_c):
    full = _dot_t0(kd_c.astype(BF16), v_c.astype(BF16))
    return jnp.concatenate(
        [full[h * GLA_DK:(h + 1) * GLA_DK, h * GLA_DV:(h + 1) * GLA_DV] for h in range(GLA_HEADS)],
        axis=0)


def _causal_conv(hbuf, cw_ref, n_rows):
    s0 = HALO - (CONV_WIDTH - 1)
    n_groups = n_rows // SUBLANES
    sub = lax.broadcasted_iota(jnp.int32, (SUBLANES, LANES), 0)
    n_base = (CONV_WIDTH + s0 - 2) // SUBLANES + 1

    def taps(phase):
        return [(a, SUBLANES * a + phase - s0) for a in range(n_base + 1)
                if 0 <= SUBLANES * a + phase - s0 < CONV_WIDTH]

    def piece(p, lanes):
        return hbuf[p * SUBLANES:(p + 1) * SUBLANES, lanes]

    def phase_sum(phase, pieces, first):
        acc = None
        for a, tap in taps(phase):
            term = pieces[a - first] * cw_ref[tap, :, lanes_of[0]]
            acc = term if acc is None else acc + term
        return acc

    strips = []
    chain = None
    lanes_of = [None]
    for lt in range(D_CONV // LANES):
        lanes = slice(lt * LANES, (lt + 1) * LANES)
        lanes_of[0] = lanes
        first_pieces = [piece(a, lanes) for a in range(n_base)]
        rolled = {r: pltpu.roll(phase_sum(r, first_pieces, 0), SUBLANES - r, 0)
                  for r in range(1, SUBLANES)}
        outs = []
        for i in range(n_groups):
            pieces = [piece(i + 1 + a, lanes) for a in range(n_base)]
            if chain is not None:
                pieces[-1] = pieces[-1] + chain
            out = phase_sum(0, pieces, 1)
            for r in range(1, SUBLANES):
                nxt = pltpu.roll(phase_sum(r, pieces, 0), SUBLANES - r, 0)
                out = out + jnp.where(sub < SUBLANES - r, rolled[r], nxt)
                rolled[r] = nxt
            outs.append(out)
            chain = _zero_after(out)
            if (lt * n_groups + i) % CONV_HOP_EVERY == 0:
                chain = pltpu.roll(chain, 1, 1)
        strips.append(jnp.concatenate(outs, axis=0))
    return jnp.concatenate(strips, axis=1)


def _gate_projection(sb, wgd_ref, gup_ref, gbias_ref):
    gd = _dot(sb, wgd_ref[...])
    z = _dot(gd.astype(BF16), gup_ref[...]) + gbias_ref[...]
    return _log_sigmoid(z) * (1.0 / GLA_TAU)


def _in_projection(s, wmain_ref, wgd_ref, gup_ref, gbias_ref):
    sb = s.astype(BF16)
    u = _dot(sb, wmain_ref[...])
    return u, _gate_projection(sb, wgd_ref, gup_ref, gbias_ref)


def _meta_kernel(meta_ref, lng_ref, lnb_ref, wmain_ref, wgd_ref, gup_ref, gbias_ref,
                 hmeta_ref, smeta_ref):
    pad = CHUNK - N_META
    s = _layer_norm(meta_ref[...], lng_ref[...], lnb_ref[...])
    s = jnp.concatenate([jnp.zeros((pad, D_MODEL), F32), s], axis=0)
    u, lg = _in_projection(s, wmain_ref, wgd_ref, gup_ref, gbias_ref)
    h = u[:, :D_CONV] * _sigmoid(u[:, D_CONV:2 * D_CONV])
    hmeta_ref[...] = h[pad:, :]
    is_meta = lax.broadcasted_iota(jnp.int32, (CHUNK, 1), 0) >= pad
    k = jnp.where(is_meta, u[:, 2 * D_CONV + D_QK:2 * D_CONV + 2 * D_QK], 0.0)
    v = jnp.where(is_meta, u[:, 2 * D_CONV + 2 * D_QK:2 * D_CONV + 2 * D_QK + D_V], 0.0)
    lg = jnp.where(is_meta, lg, 0.0)
    b = _chunk_cumsum(lg, CHUNK)
    kd = k * jnp.exp(b[CHUNK - 1:CHUNK, :] - b)
    smeta_ref[...] = _state_delta(kd, v)


def _layer_kernel(x_ref, hmeta_ref, smeta_ref, lng_ref, lnb_ref, wmain_ref, wgd_ref, gup_ref,
                  gbias_ref, cw_ref, cb_ref, clg_ref, clb_ref, gng_ref, l1g_ref, l1b_ref,
                  l2g_ref, l2b_ref, *rest, tiles_per_seq, n_tiles):
    n_out, n_up = D_MODEL // W_COLS, D_FF // W_COLS
    wout_refs, w1_refs, w2_refs = rest[:n_out], rest[n_out:n_out + n_up], rest[n_out + n_up:2 * n_out + n_up]
    o_ref, hbuf, sbuf, s1buf, ubuf, lgbuf, resbuf = rest[2 * n_out + n_up:]
    tm = TILE_M
    n_chunks = tm // CHUNK

    def tile_step(i, carry):
        step = pl.program_id(0) * PIPE_DEPTH + i
        rows_of_tile = pl.ds(pl.multiple_of(i * tm, tm), tm)

        @pl.when(step == 0)
        def _():
            s1buf[...] = jnp.zeros((tm, D_MODEL), F32)
            resbuf[...] = jnp.zeros((tm, D_MODEL), F32)
            ubuf[...] = jnp.zeros((tm, D_MAIN), F32)
            lgbuf[...] = jnp.zeros((tm, D_QK), F32)

        @pl.when(jnp.clip(step - 1, 0, n_tiles - 1) % tiles_per_seq == 0)
        def _():
            hbuf[0:HALO - N_META, :] = jnp.zeros((HALO - N_META, D_CONV), F32)
            hbuf[HALO - N_META:HALO, :] = hmeta_ref[...]
            sbuf[...] = smeta_ref[...]

        s1_prev = s1buf[...]
        s1b = s1_prev.astype(BF16)
        u = ubuf[...]
        lg = lgbuf[...]
        s = resbuf[...]
        q = u[:, 2 * D_CONV:2 * D_CONV + D_QK]
        k = u[:, 2 * D_CONV + D_QK:2 * D_CONV + 2 * D_QK]
        v = u[:, 2 * D_CONV + 2 * D_QK:2 * D_CONV + 2 * D_QK + D_V]
        r = u[:, 2 * D_CONV + 2 * D_QK + D_V:D_MAIN]

        def up_block(c):
            hb = jnp.maximum(_dot(s1b, w1_refs[c][...]), 0.0)
            return (hb * hb).astype(BF16)

        b = _chunk_cumsum(lg, tm)
        total_cols = _chunk_total_cols(lg, tm)
        hid2 = [up_block(0), up_block(1)]

        s_new = _layer_norm(x_ref[rows_of_tile, :], lng_ref[...], lnb_ref[...])
        sb_new = s_new.astype(BF16)
        lg_new = _gate_projection(sb_new, wgd_ref, gup_ref, gbias_ref)

        qe = q * (GLA_DK ** -0.5) * jnp.exp(b)
        ke = (k * jnp.exp(-b)).astype(BF16)
        vb = v.astype(BF16)
        ri = lax.broadcasted_iota(jnp.int32, (tm, tm), 0)
        ci = lax.broadcasted_iota(jnp.int32, (tm, tm), 1)
        causal = (ri // CHUNK == ci // CHUNK) & (ci <= ri)
        lane_head = lax.broadcasted_iota(jnp.int32, (1, D_QK), 1) // GLA_DK
        qe_heads = [jnp.where(lane_head == h, qe, 0.0).astype(BF16) for h in range(GLA_HEADS)]
        scores = [_dot_t1(qe_heads[h], ke) for h in range(GLA_HEADS)]
        deltas = []
        for c in range(n_chunks):
            rows = slice(c * CHUNK, (c + 1) * CHUNK)
            b_c = b[rows, :]
            kd = k[rows, :] * jnp.exp(b_c[CHUNK - 1:CHUNK, :] - b_c)
            deltas.append(_state_delta(kd, v[rows, :]))
        hid2 += [up_block(2), up_block(3)]

        zero_v = jnp.zeros((tm, GLA_DV), BF16)
        o_intra = []
        for h in range(0, GLA_HEADS, 2):
            a_pair = jnp.concatenate(
                [jnp.where(causal, scores[h + d], 0.0).astype(BF16) for d in range(2)], axis=1)
            v0, v1 = (vb[:, (h + d) * GLA_DV:(h + d + 1) * GLA_DV] for d in range(2))
            v_pair = jnp.concatenate([jnp.concatenate([v0, zero_v], axis=1),
                                      jnp.concatenate([zero_v, v1], axis=1)], axis=0)
            o_intra.append(_dot(a_pair, v_pair))
        states = []
        state = sbuf[...]
        for c in range(n_chunks):
            states.append(state.astype(BF16))
            state = jnp.exp(total_cols[:, c * LANES:(c + 1) * LANES]) * state + deltas[c]
        hid2 += [up_block(4), up_block(5)]

        qe_b = qe.astype(BF16)
        zero_s = jnp.zeros((GLA_DK, GLA_DV), BF16)
        o_inter = []
        for c in range(n_chunks):
            s_bd = jnp.concatenate(
                [jnp.concatenate([states[c][h * GLA_DK:(h + 1) * GLA_DK, :] if g == h else zero_s
                                  for g in range(GLA_HEADS)], axis=1) for h in range(GLA_HEADS)], axis=0)
            o_inter.append(_dot(qe_b[c * CHUNK:(c + 1) * CHUNK, :], s_bd))
        hid2 += [up_block(6), up_block(7)]

        hid2 = jnp.concatenate(hid2, axis=1)
        f = jnp.concatenate([_dot(hid2, w[...]) for w in w2_refs], axis=1)

        hbuf[HALO:HALO + tm, :] = u[:, :D_CONV] * _sigmoid(u[:, D_CONV:2 * D_CONV])
        conv = _causal_conv(hbuf, cw_ref, tm) + cb_ref[...]
        halo = hbuf[tm:tm + HALO, :]
        conv_out = _silu(_layer_norm(conv, clg_ref[...], clb_ref[...]))
        o_all = jnp.concatenate(o_intra, axis=1) + jnp.concatenate(o_inter, axis=0)
        gla_blocks = []
        for h in range(GLA_HEADS):
            o = o_all[:, h * GLA_DV:(h + 1) * GLA_DV]
            o = o * lax.rsqrt(jnp.mean(o * o, axis=-1, keepdims=True) + LN_EPS) * gng_ref[...]
            gla_blocks.append(o * _silu(r[:, h * GLA_DV:(h + 1) * GLA_DV]))

        mixin = jnp.concatenate([conv_out] + gla_blocks, axis=-1).astype(BF16)
        mix = jnp.concatenate([_dot(mixin, w[...]) for w in wout_refs], axis=1)
        s1_new = _layer_norm(DEEPNORM_ALPHA * s + mix, l1g_ref[...], l1b_ref[...])

        out = _layer_norm(DEEPNORM_ALPHA * s1_prev + f, l2g_ref[...], l2b_ref[...])

        u_new = _dot(sb_new, wmain_ref[...])
        hbuf[0:HALO, :] = halo
        sbuf[...] = state
        s1buf[...] = s1_new
        o_ref[rows_of_tile, :] = out
        resbuf[...] = s_new
        ubuf[...] = u_new
        lgbuf[...] = lg_new
        return carry

    lax.fori_loop(0, PIPE_DEPTH, tile_step, 0)


def _resident(shape):
    return pl.BlockSpec(shape, lambda *_: (0,) * len(shape), pipeline_mode=pl.Buffered(1))


def _resident_col_blocks(w):
    kdim, ndim = w.shape
    return [pl.BlockSpec((kdim, W_COLS), lambda *_, c=c: (0, c), pipeline_mode=pl.Buffered(1))
            for c in range(ndim // W_COLS)]


def kernel(x, meta_tokens, ln_in_g, ln_in_b, w_in, conv_w, conv_b, conv_ln_g, conv_ln_b,
           gate_up, gate_bias, gla_norm_g, w_out, ln1_g, ln1_b, w_ff1, w_ff2, ln2_g, ln2_b):
    batch, seq, d_model = x.shape
    assert d_model == D_MODEL and seq % (PIPE_DEPTH * TILE_M) == 0 and w_in.shape[0] == DEPTH
    row = lambda a: a.reshape(1, -1).astype(F32)

    w_main = w_in[0, :, :D_MAIN].astype(BF16)
    w_gd = jnp.pad(w_in[0, :, D_MAIN:], ((0, 0), (0, LANES - GLA_GATE_RANK))).astype(BF16)
    gup = jnp.pad(gate_up[0], ((0, LANES - GLA_GATE_RANK), (0, 0))).astype(BF16)
    lng, lnb, gbias = row(ln_in_g), row(ln_in_b), row(gate_bias[0])

    conv_taps = jnp.broadcast_to(conv_w[0].astype(F32)[:, None, :], (CONV_WIDTH, SUBLANES, D_CONV))

    proj_args = (lng, lnb, w_main, w_gd, gup, gbias)
    h_meta, s_meta = pl.pallas_call(
        _meta_kernel,
        out_shape=(jax.ShapeDtypeStruct((N_META, D_CONV), F32),
                   jax.ShapeDtypeStruct((D_QK, GLA_DV), F32)),
        compiler_params=pltpu.CompilerParams(vmem_limit_bytes=VMEM_LIMIT_BYTES),
        name="meta_state",
    )(meta_tokens.astype(F32), *proj_args)

    layer_args = (
        h_meta, s_meta, *proj_args,
        conv_taps, row(conv_b[0]), row(conv_ln_g[0]), row(conv_ln_b[0]),
        row(gla_norm_g[0]), row(ln1_g[0]), row(ln1_b[0]), row(ln2_g[0]), row(ln2_b[0]),
    )
    blocked = (w_out[0].astype(BF16), w_ff1[0].astype(BF16), w_ff2[0].astype(BF16))
    blocked_specs = [spec for w in blocked for spec in _resident_col_blocks(w)]
    blocked_args = [w for w in blocked for _ in range(w.shape[1] // W_COLS)]
    tiles_per_seq = seq // TILE_M
    n_tiles = batch * tiles_per_seq

    blocks_per_seq = tiles_per_seq // PIPE_DEPTH
    n_blocks = batch * blocks_per_seq

    def in_block(g):
        j = jnp.minimum(g, n_blocks - 1)
        return (j // blocks_per_seq, j % blocks_per_seq, 0)

    def out_block(g):
        j = jnp.maximum(g - 1, 0)
        return (j // blocks_per_seq, j % blocks_per_seq, 0)

    return pl.pallas_call(
        functools.partial(_layer_kernel, tiles_per_seq=tiles_per_seq, n_tiles=n_tiles),
        grid=(n_blocks + 1,),
        in_specs=[pl.BlockSpec((None, PIPE_DEPTH * TILE_M, D_MODEL), in_block)]
        + [_resident(a.shape) for a in layer_args] + blocked_specs,
        out_specs=pl.BlockSpec((None, PIPE_DEPTH * TILE_M, D_MODEL), out_block),
        out_shape=jax.ShapeDtypeStruct(x.shape, x.dtype),
        scratch_shapes=[pltpu.VMEM((HALO + TILE_M, D_CONV), F32),
                        pltpu.VMEM((D_QK, GLA_DV), F32),
                        pltpu.VMEM((TILE_M, D_MODEL), F32),
                        pltpu.VMEM((TILE_M, D_MAIN), F32),
                        pltpu.VMEM((TILE_M, D_QK), F32),
                        pltpu.VMEM((TILE_M, D_MODEL), F32)],
        compiler_params=pltpu.CompilerParams(
            dimension_semantics=("arbitrary",),
            vmem_limit_bytes=VMEM_LIMIT_BYTES),
        name="layer",
    )(x, *layer_args, *blocked_args)
```

```python
import functools

import jax
import jax.numpy as jnp
from jax import lax
from jax.experimental import pallas as pl
from jax.experimental.pallas import tpu as pltpu

D_MODEL = 1024
N_META = 16
D_CONV = 512
CONV_WIDTH = 31
GLA_HEADS = 4
GLA_DV = 128
GLA_DK = 64
GLA_GATE_RANK = 16
GLA_TAU = 16.0
CHUNK = 64
D_FF = 4096
LN_EPS = 1e-5
DEPTH = 1
DEEPNORM_ALPHA = (2.0 * DEPTH) ** 0.25

D_QK = GLA_HEADS * GLA_DK
D_V = GLA_HEADS * GLA_DV
D_MAIN = 2 * D_CONV + 2 * D_QK + 2 * D_V
LANES = 128
SUBLANES = 8
HALO = 32
CONV_HOP_EVERY = 3
W_COLS = 512
PIPE_DEPTH = 2
TILE_M = 256
VMEM_LIMIT_BYTES = 56 * 1024 * 1024

F32 = jnp.float32
BF16 = jnp.bfloat16


def _dot(a, b):
    return jnp.dot(a, b, preferred_element_type=F32)


def _dot_t0(a, b):
    return lax.dot_general(a, b, (((0,), (0,)), ((), ())), preferred_element_type=F32)


def _dot_t1(a, b):
    return lax.dot_general(a, b, (((1,), (1,)), ((), ())), preferred_element_type=F32)


def _layer_norm(x, g, b):
    mu = jnp.mean(x, axis=-1, keepdims=True)
    xc = x - mu
    var = jnp.mean(xc * xc, axis=-1, keepdims=True)
    return xc * lax.rsqrt(var + LN_EPS) * g + b


def _sigmoid(x):
    return 1.0 / (1.0 + jnp.exp(-x))


def _silu(x):
    return x * _sigmoid(x)


def _log_sigmoid(z):
    return jnp.minimum(z, 0.0) - jnp.log(1.0 + jnp.exp(-jnp.abs(z)))


def _zero_after(x):
    bits = lax.bitcast_convert_type(x, jnp.uint32)
    half = jnp.uint32(16)
    zero_bits = lax.shift_right_logical(lax.shift_right_logical(bits, half), half)
    return lax.bitcast_convert_type(zero_bits, F32)


def _split_hi_lo(x):
    hi = x.astype(BF16)
    lo = (x - hi.astype(F32)).astype(BF16)
    return hi, lo


def _chunk_cumsum(lg, rows):
    ri = lax.broadcasted_iota(jnp.int32, (rows, rows), 0)
    ci = lax.broadcasted_iota(jnp.int32, (rows, rows), 1)
    tri = jnp.where((ri // CHUNK == ci // CHUNK) & (ci <= ri), 1.0, 0.0).astype(BF16)
    hi, lo = _split_hi_lo(lg)
    return _dot(tri, hi) + _dot(tri, lo)


def _chunk_total_cols(lg, rows):
    n_chunks = rows // CHUNK
    ri = lax.broadcasted_iota(jnp.int32, (rows, n_chunks * LANES), 0)
    ci = lax.broadcasted_iota(jnp.int32, (rows, n_chunks * LANES), 1)
    ind = jnp.where(ri // CHUNK == ci // LANES, 1.0, 0.0).astype(BF16)
    hi, lo = _split_hi_lo(lg)
    return _dot_t0(hi, ind) + _dot_t0(lo, ind)


def _state_delta(kd_c, v_c):
    full = _dot_t0(kd_c.astype(BF16), v_c.astype(BF16))
    return jnp.concatenate(
        [full[h * GLA_DK:(h + 1) * GLA_DK, h * GLA_DV:(h + 1) * GLA_DV] for h in range(GLA_HEADS)],
        axis=0)


def _causal_conv(hbuf, cw_ref, n_rows):
    s0 = HALO - (CONV_WIDTH - 1)
    n_groups = n_rows // SUBLANES
    sub = lax.broadcasted_iota(jnp.int32, (SUBLANES, LANES), 0)
    n_base = (CONV_WIDTH + s0 - 2) // SUBLANES + 1

    def taps(phase):
        return [(a, SUBLANES * a + phase - s0) for a in range(n_base + 1)
                if 0 <= SUBLANES * a + phase - s0 < CONV_WIDTH]

    def piece(p, lanes):
        return hbuf[p * SUBLANES:(p + 1) * SUBLANES, lanes]

    def phase_sum(phase, pieces, first):
        acc = None
        for a, tap in taps(phase):
            term = pieces[a - first] * cw_ref[tap, :, lanes_of[0]]
            acc = term if acc is None else acc + term
        return acc

    strips = []
    chain = None
    lanes_of = [None]
    for lt in range(D_CONV // LANES):
        lanes = slice(lt * LANES, (lt + 1) * LANES)
        lanes_of[0] = lanes
        first_pieces = [piece(a, lanes) for a in range(n_base)]
        rolled = {r: pltpu.roll(phase_sum(r, first_pieces, 0), SUBLANES - r, 0)
                  for r in range(1, SUBLANES)}
        outs = []
        for i in range(n_groups):
            pieces = [piece(i + 1 + a, lanes) for a in range(n_base)]
            if chain is not None:
                pieces[-1] = pieces[-1] + chain
            out = phase_sum(0, pieces, 1)
            for r in range(1, SUBLANES):
                nxt = pltpu.roll(phase_sum(r, pieces, 0), SUBLANES - r, 0)
                out = out + jnp.where(sub < SUBLANES - r, rolled[r], nxt)
                rolled[r] = nxt
            outs.append(out)
            chain = _zero_after(out)
            if (lt * n_groups + i) % CONV_HOP_EVERY == 0:
                chain = pltpu.roll(chain, 1, 1)
        strips.append(jnp.concatenate(outs, axis=0))
    return jnp.concatenate(strips, axis=1)


def _gate_projection(sb, wgd_ref, gup_ref, gbias_ref):
    gd = _dot(sb, wgd_ref[...])
    z = _dot(gd.astype(BF16), gup_ref[...]) + gbias_ref[...]
    return _log_sigmoid(z) * (1.0 / GLA_TAU)


def _in_projection(s, wmain_ref, wgd_ref, gup_ref, gbias_ref):
    sb = s.astype(BF16)
    u = _dot(sb, wmain_ref[...])
    return u, _gate_projection(sb, wgd_ref, gup_ref, gbias_ref)


def _meta_kernel(meta_ref, lng_ref, lnb_ref, wmain_ref, wgd_ref, gup_ref, gbias_ref,
                 hmeta_ref, smeta_ref):
    pad = CHUNK - N_META
    s = _layer_norm(meta_ref[...], lng_ref[...], lnb_ref[...])
    s = jnp.concatenate([jnp.zeros((pad, D_MODEL), F32), s], axis=0)
    u, lg = _in_projection(s, wmain_ref, wgd_ref, gup_ref, gbias_ref)
    h = u[:, :D_CONV] * _sigmoid(u[:, D_CONV:2 * D_CONV])
    hmeta_ref[...] = h[pad:, :]
    is_meta = lax.broadcasted_iota(jnp.int32, (CHUNK, 1), 0) >= pad
    k = jnp.where(is_meta, u[:, 2 * D_CONV + D_QK:2 * D_CONV + 2 * D_QK], 0.0)
    v = jnp.where(is_meta, u[:, 2 * D_CONV + 2 * D_QK:2 * D_CONV + 2 * D_QK + D_V], 0.0)
    lg = jnp.where(is_meta, lg, 0.0)
    b = _chunk_cumsum(lg, CHUNK)
    kd = k * jnp.exp(b[CHUNK - 1:CHUNK, :] - b)
    smeta_ref[...] = _state_delta(kd, v)


def _layer_kernel(x_ref, hmeta_ref, smeta_ref, lng_ref, lnb_ref, wmain_ref, wgd_ref, gup_ref,
                  gbias_ref, cw_ref, cb_ref, clg_ref, clb_ref, gng_ref, l1g_ref, l1b_ref,
                  l2g_ref, l2b_ref, *rest, tiles_per_seq, n_tiles):
    n_out, n_up = D_MODEL // W_COLS, D_FF // W_COLS
    wout_refs, w1_refs, w2_refs = rest[:n_out], rest[n_out:n_out + n_up], rest[n_out + n_up:2 * n_out + n_up]
    o_ref, hbuf, sbuf, s1buf, ubuf, lgbuf, resbuf = rest[2 * n_out + n_up:]
    tm = TILE_M
    n_chunks = tm // CHUNK

    def tile_step(i, carry):
        step = pl.program_id(0) * PIPE_DEPTH + i
        rows_of_tile = pl.ds(pl.multiple_of(i * tm, tm), tm)

        @pl.when(step == 0)
        def _():
            s1buf[...] = jnp.zeros((tm, D_MODEL), F32)
            resbuf[...] = jnp.zeros((tm, D_MODEL), F32)
            ubuf[...] = jnp.zeros((tm, D_MAIN), F32)
            lgbuf[...] = jnp.zeros((tm, D_QK), F32)

        @pl.when(jnp.clip(step - 1, 0, n_tiles - 1) % tiles_per_seq == 0)
        def _():
            hbuf[0:HALO - N_META, :] = jnp.zeros((HALO - N_META, D_CONV), F32)
            hbuf[HALO - N_META:HALO, :] = hmeta_ref[...]
            sbuf[...] = smeta_ref[...]

        s1_prev = s1buf[...]
        s1b = s1_prev.astype(BF16)
        u = ubuf[...]
        lg = lgbuf[...]
        s = resbuf[...]
        q = u[:, 2 * D_CONV:2 * D_CONV + D_QK]
        k = u[:, 2 * D_CONV + D_QK:2 * D_CONV + 2 * D_QK]
        v = u[:, 2 * D_CONV + 2 * D_QK:2 * D_CONV + 2 * D_QK + D_V]
        r = u[:, 2 * D_CONV + 2 * D_QK + D_V:D_MAIN]

        def up_block(c):
            hb = jnp.maximum(_dot(s1b, w1_refs[c][...]), 0.0)
            return (hb * hb).astype(BF16)

        b = _chunk_cumsum(lg, tm)
        total_cols = _chunk_total_cols(lg, tm)
        hid2 = [up_block(0), up_block(1)]

        s_new = _layer_norm(x_ref[rows_of_tile, :], lng_ref[...], lnb_ref[...])
        sb_new = s_new.astype(BF16)
        lg_new = _gate_projection(sb_new, wgd_ref, gup_ref, gbias_ref)

        qe = q * (GLA_DK ** -0.5) * jnp.exp(b)
        ke = (k * jnp.exp(-b)).astype(BF16)
        vb = v.astype(BF16)
        ri = lax.broadcasted_iota(jnp.int32, (tm, tm), 0)
        ci = lax.broadcasted_iota(jnp.int32, (tm, tm), 1)
        causal = (ri // CHUNK == ci // CHUNK) & (ci <= ri)
        lane_head = lax.broadcasted_iota(jnp.int32, (1, D_QK), 1) // GLA_DK
        qe_heads = [jnp.where(lane_head == h, qe, 0.0).astype(BF16) for h in range(GLA_HEADS)]
        scores = [_dot_t1(qe_heads[h], ke) for h in range(GLA_HEADS)]
        deltas = []
        for c in range(n_chunks):
            rows = slice(c * CHUNK, (c + 1) * CHUNK)
            b_c = b[rows, :]
            kd = k[rows, :] * jnp.exp(b_c[CHUNK - 1:CHUNK, :] - b_c)
            deltas.append(_state_delta(kd, v[rows, :]))
        hid2 += [up_block(2), up_block(3)]

        zero_v = jnp.zeros((tm, GLA_DV), BF16)
        o_intra = []
        for h in range(0, GLA_HEADS, 2):
            a_pair = jnp.concatenate(
                [jnp.where(causal, scores[h + d], 0.0).astype(BF16) for d in range(2)], axis=1)
            v0, v1 = (vb[:, (h + d) * GLA_DV:(h + d + 1) * GLA_DV] for d in range(2))
            v_pair = jnp.concatenate([jnp.concatenate([v0, zero_v], axis=1),
                                      jnp.concatenate([zero_v, v1], axis=1)], axis=0)
            o_intra.append(_dot(a_pair, v_pair))
        states = []
        state = sbuf[...]
        for c in range(n_chunks):
            states.append(state.astype(BF16))
            state = jnp.exp(total_cols[:, c * LANES:(c + 1) * LANES]) * state + deltas[c]
        hid2 += [up_block(4), up_block(5)]

        qe_b = qe.astype(BF16)
        zero_s = jnp.zeros((GLA_DK, GLA_DV), BF16)
        o_inter = []
        for c in range(n_chunks):
            s_bd = jnp.concatenate(
                [jnp.concatenate([states[c][h * GLA_DK:(h + 1) * GLA_DK, :] if g == h else zero_s
                                  for g in range(GLA_HEADS)], axis=1) for h in range(GLA_HEADS)], axis=0)
            o_inter.append(_dot(qe_b[c * CHUNK:(c + 1) * CHUNK, :], s_bd))
        hid2 += [up_block(6), up_block(7)]

        hid2 = jnp.concatenate(hid2, axis=1)
        f = jnp.concatenate([_dot(hid2, w[...]) for w in w2_refs], axis=1)

        hbuf[HALO:HALO + tm, :] = u[:, :D_CONV] * _sigmoid(u[:, D_CONV:2 * D_CONV])
        conv = _causal_conv(hbuf, cw_ref, tm) + cb_ref[...]
        halo = hbuf[tm:tm + HALO, :]
        conv_out = _silu(_layer_norm(conv, clg_ref[...], clb_ref[...]))
        o_all = jnp.concatenate(o_intra, axis=1) + jnp.concatenate(o_inter, axis=0)
        gla_blocks = []
        for h in range(GLA_HEADS):
            o = o_all[:, h * GLA_DV:(h + 1) * GLA_DV]
            o = o * lax.rsqrt(jnp.mean(o * o, axis=-1, keepdims=True) + LN_EPS) * gng_ref[...]
            gla_blocks.append(o * _silu(r[:, h * GLA_DV:(h + 1) * GLA_DV]))

        mixin = jnp.concatenate([conv_out] + gla_blocks, axis=-1).astype(BF16)
        mix = jnp.concatenate([_dot(mixin, w[...]) for w in wout_refs], axis=1)
        s1_new = _layer_norm(DEEPNORM_ALPHA * s + mix, l1g_ref[...], l1b_ref[...])

        out = _layer_norm(DEEPNORM_ALPHA * s1_prev + f, l2g_ref[...], l2b_ref[...])

        u_new = _dot(sb_new, wmain_ref[...])
        hbuf[0:HALO, :] = halo
        sbuf[...] = state
        s1buf[...] = s1_new
        o_ref[rows_of_tile, :] = out
        resbuf[...] = s_new
        ubuf[...] = u_new
        lgbuf[...] = lg_new
        return carry

    lax.fori_loop(0, PIPE_DEPTH, tile_step, 0, unroll=True)


def _resident(shape):
    return pl.BlockSpec(shape, lambda *_: (0,) * len(shape), pipeline_mode=pl.Buffered(1))


def _resident_col_blocks(w):
    kdim, ndim = w.shape
    return [pl.BlockSpec((kdim, W_COLS), lambda *_, c=c: (0, c), pipeline_mode=pl.Buffered(1))
            for c in range(ndim // W_COLS)]


def kernel(x, meta_tokens, ln_in_g, ln_in_b, w_in, conv_w, conv_b, conv_ln_g, conv_ln_b,
           gate_up, gate_bias, gla_norm_g, w_out, ln1_g, ln1_b, w_ff1, w_ff2, ln2_g, ln2_b):
    batch, seq, d_model = x.shape
    assert d_model == D_MODEL and seq % (PIPE_DEPTH * TILE_M) == 0 and w_in.shape[0] == DEPTH
    row = lambda a: a.reshape(1, -1).astype(F32)

    w_main = w_in[0, :, :D_MAIN].astype(BF16)
    w_gd = jnp.pad(w_in[0, :, D_MAIN:], ((0, 0), (0, LANES - GLA_GATE_RANK))).astype(BF16)
    gup = jnp.pad(gate_up[0], ((0, LANES - GLA_GATE_RANK), (0, 0))).astype(BF16)
    lng, lnb, gbias = row(ln_in_g), row(ln_in_b), row(gate_bias[0])

    conv_taps = jnp.broadcast_to(conv_w[0].astype(F32)[:, None, :], (CONV_WIDTH, SUBLANES, D_CONV))

    proj_args = (lng, lnb, w_main, w_gd, gup, gbias)
    h_meta, s_meta = pl.pallas_call(
        _meta_kernel,
        out_shape=(jax.ShapeDtypeStruct((N_META, D_CONV), F32),
                   jax.ShapeDtypeStruct((D_QK, GLA_DV), F32)),
        compiler_params=pltpu.CompilerParams(vmem_limit_bytes=VMEM_LIMIT_BYTES),
        name="meta_state",
    )(meta_tokens.astype(F32), *proj_args)

    layer_args = (
        h_meta, s_meta, *proj_args,
        conv_taps, row(conv_b[0]), row(conv_ln_g[0]), row(conv_ln_b[0]),
        row(gla_norm_g[0]), row(ln1_g[0]), row(ln1_b[0]), row(ln2_g[0]), row(ln2_b[0]),
    )
    blocked = (w_out[0].astype(BF16), w_ff1[0].astype(BF16), w_ff2[0].astype(BF16))
    blocked_specs = [spec for w in blocked for spec in _resident_col_blocks(w)]
    blocked_args = [w for w in blocked for _ in range(w.shape[1] // W_COLS)]
    tiles_per_seq = seq // TILE_M
    n_tiles = batch * tiles_per_seq

    blocks_per_seq = tiles_per_seq // PIPE_DEPTH
    n_blocks = batch * blocks_per_seq

    def in_block(g):
        j = jnp.minimum(g, n_blocks - 1)
        return (j // blocks_per_seq, j % blocks_per_seq, 0)

    def out_block(g):
        j = jnp.maximum(g - 1, 0)
        return (j // blocks_per_seq, j % blocks_per_seq, 0)

    return pl.pallas_call(
        functools.partial(_layer_kernel, tiles_per_seq=tiles_per_seq, n_tiles=n_tiles),
        grid=(n_blocks + 1,),
        in_specs=[pl.BlockSpec((None, PIPE_DEPTH * TILE_M, D_MODEL), in_block)]
        + [_resident(a.shape) for a in layer_args] + blocked_specs,
        out_specs=pl.BlockSpec((None, PIPE_DEPTH * TILE_M, D_MODEL), out_block),
        out_shape=jax.ShapeDtypeStruct(x.shape, x.dtype),
        scratch_shapes=[pltpu.VMEM((HALO + TILE_M, D_CONV), F32),
                        pltpu.VMEM((D_QK, GLA_DV), F32),
                        pltpu.VMEM((TILE_M, D_MODEL), F32),
                        pltpu.VMEM((TILE_M, D_MAIN), F32),
                        pltpu.VMEM((TILE_M, D_QK), F32),
                        pltpu.VMEM((TILE_M, D_MODEL), F32)],
        compiler_params=pltpu.CompilerParams(
            dimension_semantics=("arbitrary",),
            vmem_limit_bytes=VMEM_LIMIT_BYTES),
        name="layer",
    )(x, *layer_args, *blocked_args)
```

```python
import functools

import jax
import jax.numpy as jnp
from jax import lax
from jax.experimental import pallas as pl
from jax.experimental.pallas import tpu as pltpu

D_MODEL = 1024
N_META = 16
D_CONV = 512
CONV_WIDTH = 31
GLA_HEADS = 4
GLA_DV = 128
GLA_DK = 64
GLA_GATE_RANK = 16
GLA_TAU = 16.0
CHUNK = 64
D_FF = 4096
LN_EPS = 1e-5
DEPTH = 1
DEEPNORM_ALPHA = (2.0 * DEPTH) ** 0.25

D_QK = GLA_HEADS * GLA_DK
D_V = GLA_HEADS * GLA_DV
D_MAIN = 2 * D_CONV + 2 * D_QK + 2 * D_V
LANES = 128
SUBLANES = 8
HALO = 32
CONV_HOP_EVERY = 3
W_COLS = 512
STAGE_ROWS = 128
PIPE_DEPTH = 2
TILE_M = 256
VMEM_LIMIT_BYTES = 56 * 1024 * 1024

F32 = jnp.float32
BF16 = jnp.bfloat16


def _dot(a, b):
    return jnp.dot(a, b, preferred_element_type=F32)


def _dot_t0(a, b):
    return lax.dot_general(a, b, (((0,), (0,)), ((), ())), preferred_element_type=F32)


def _dot_t1(a, b):
    return lax.dot_general(a, b, (((1,), (1,)), ((), ())), preferred_element_type=F32)


def _layer_norm(x, g, b):
    mu = jnp.mean(x, axis=-1, keepdims=True)
    xc = x - mu
    var = jnp.mean(xc * xc, axis=-1, keepdims=True)
    return xc * lax.rsqrt(var + LN_EPS) * g + b


def _sigmoid(x):
    return 1.0 / (1.0 + jnp.exp(-x))


def _silu(x):
    return x * _sigmoid(x)


def _log_sigmoid(z):
    return jnp.minimum(z, 0.0) - jnp.log(1.0 + jnp.exp(-jnp.abs(z)))


def _zero_after(x):
    bits = lax.bitcast_convert_type(x, jnp.uint32)
    half = jnp.uint32(16)
    zero_bits = lax.shift_right_logical(lax.shift_right_logical(bits, half), half)
    return lax.bitcast_convert_type(zero_bits, F32)


def _split_hi_lo(x):
    hi = x.astype(BF16)
    lo = (x - hi.astype(F32)).astype(BF16)
    return hi, lo


def _chunk_cumsum(lg, rows):
    ri = lax.broadcasted_iota(jnp.int32, (rows, rows), 0)
    ci = lax.broadcasted_iota(jnp.int32, (rows, rows), 1)
    tri = jnp.where((ri // CHUNK == ci // CHUNK) & (ci <= ri), 1.0, 0.0).astype(BF16)
    hi, lo = _split_hi_lo(lg)
    return _dot(tri, hi) + _dot(tri, lo)


def _chunk_total_cols(lg, rows):
    n_chunks = rows // CHUNK
    ri = lax.broadcasted_iota(jnp.int32, (rows, n_chunks * LANES), 0)
    ci = lax.broadcasted_iota(jnp.int32, (rows, n_chunks * LANES), 1)
    ind = jnp.where(ri // CHUNK == ci // LANES, 1.0, 0.0).astype(BF16)
    hi, lo = _split_hi_lo(lg)
    return _dot_t0(hi, ind) + _dot_t0(lo, ind)


def _state_delta(kd_c, v_c):
    full = _dot_t0(kd_c.astype(BF16), v_c.astype(BF16))
    return jnp.concatenate(
        [full[h * GLA_DK:(h + 1) * GLA_DK, h * GLA_DV:(h + 1) * GLA_DV] for h in range(GLA_HEADS)],
        axis=0)


def _causal_conv(hbuf, cw_ref, n_rows):
    s0 = HALO - (CONV_WIDTH - 1)
    n_groups = n_rows // SUBLANES
    sub = lax.broadcasted_iota(jnp.int32, (SUBLANES, LANES), 0)
    n_base = (CONV_WIDTH + s0 - 2) // SUBLANES + 1

    def taps(phase):
        return [(a, SUBLANES * a + phase - s0) for a in range(n_base + 1)
                if 0 <= SUBLANES * a + phase - s0 < CONV_WIDTH]

    def piece(p, lanes):
        return hbuf[p * SUBLANES:(p + 1) * SUBLANES, lanes]

    def phase_sum(phase, pieces, first):
        acc = None
        for a, tap in taps(phase):
            term = pieces[a - first] * cw_ref[tap, :, lanes_of[0]]
            acc = term if acc is None else acc + term
        return acc

    strips = []
    chain = None
    lanes_of = [None]
    for lt in range(D_CONV // LANES):
        lanes = slice(lt * LANES, (lt + 1) * LANES)
        lanes_of[0] = lanes
        first_pieces = [piece(a, lanes) for a in range(n_base)]
        rolled = {r: pltpu.roll(phase_sum(r, first_pieces, 0), SUBLANES - r, 0)
                  for r in range(1, SUBLANES)}
        outs = []
        for i in range(n_groups):
            pieces = [piece(i + 1 + a, lanes) for a in range(n_base)]
            if chain is not None:
                pieces[-1] = pieces[-1] + chain
            out = phase_sum(0, pieces, 1)
            for r in range(1, SUBLANES):
                nxt = pltpu.roll(phase_sum(r, pieces, 0), SUBLANES - r, 0)
                out = out + jnp.where(sub < SUBLANES - r, rolled[r], nxt)
                rolled[r] = nxt
            outs.append(out)
            chain = _zero_after(out)
            if (lt * n_groups + i) % CONV_HOP_EVERY == 0:
                chain = pltpu.roll(chain, 1, 1)
        strips.append(jnp.concatenate(outs, axis=0))
    return jnp.concatenate(strips, axis=1)


def _gate_projection(sb, wgd_ref, gup_ref, gbias_ref):
    gd = _dot(sb, wgd_ref[...])
    z = _dot(gd.astype(BF16), gup_ref[...]) + gbias_ref[...]
    return _log_sigmoid(z) * (1.0 / GLA_TAU)


def _in_projection(s, wmain_ref, wgd_ref, gup_ref, gbias_ref):
    sb = s.astype(BF16)
    u = _dot(sb, wmain_ref[...])
    return u, _gate_projection(sb, wgd_ref, gup_ref, gbias_ref)


def _meta_kernel(meta_ref, lng_ref, lnb_ref, wmain_ref, wgd_ref, gup_ref, gbias_ref,
                 hmeta_ref, smeta_ref):
    pad = CHUNK - N_META
    s = _layer_norm(meta_ref[...], lng_ref[...], lnb_ref[...])
    s = jnp.concatenate([jnp.zeros((pad, D_MODEL), F32), s], axis=0)
    u, lg = _in_projection(s, wmain_ref, wgd_ref, gup_ref, gbias_ref)
    h = u[:, :D_CONV] * _sigmoid(u[:, D_CONV:2 * D_CONV])
    hmeta_ref[...] = h[pad:, :]
    is_meta = lax.broadcasted_iota(jnp.int32, (CHUNK, 1), 0) >= pad
    k = jnp.where(is_meta, u[:, 2 * D_CONV + D_QK:2 * D_CONV + 2 * D_QK], 0.0)
    v = jnp.where(is_meta, u[:, 2 * D_CONV + 2 * D_QK:2 * D_CONV + 2 * D_QK + D_V], 0.0)
    lg = jnp.where(is_meta, lg, 0.0)
    b = _chunk_cumsum(lg, CHUNK)
    kd = k * jnp.exp(b[CHUNK - 1:CHUNK, :] - b)
    smeta_ref[...] = _state_delta(kd, v)


def _load_bf16_col_blocks(w_hbm, blocks, stage, sem):
    n_chunks = w_hbm.shape[0] // STAGE_ROWS

    def chunk_copy(c, slot):
        return pltpu.make_async_copy(w_hbm.at[pl.ds(c * STAGE_ROWS, STAGE_ROWS), :],
                                     stage.at[slot], sem.at[slot])

    chunk_copy(0, 0).start()

    def convert_chunk(c, carry):
        slot = c % 2

        @pl.when(c + 1 < n_chunks)
        def _():
            chunk_copy(c + 1, 1 - slot).start()

        chunk_copy(c, slot).wait()
        rows = pl.ds(pl.multiple_of(c * STAGE_ROWS, STAGE_ROWS), STAGE_ROWS)
        for b, blk in enumerate(blocks):
            blk[rows, :] = stage[slot, :, b * W_COLS:(b + 1) * W_COLS].astype(BF16)
        return carry

    lax.fori_loop(0, n_chunks, convert_chunk, 0)


def _layer_kernel(x_ref, hmeta_ref, smeta_ref, lng_ref, lnb_ref, wmain_ref, wgd_ref, gup_ref,
                  gbias_ref, cw_ref, cb_ref, clg_ref, clb_ref, gng_ref, l1g_ref, l1b_ref,
                  l2g_ref, l2b_ref, *rest, tiles_per_seq, n_tiles):
    n_out, n_up = D_MODEL // W_COLS, D_FF // W_COLS
    wout_hbm, w1_hbm, w2_hbm, o_ref, hbuf, sbuf, s1buf, ubuf, lgbuf, resbuf = rest[:10]
    blocks = rest[10:10 + 2 * n_out + n_up]
    wout_refs, w1_refs, w2_refs = blocks[:n_out], blocks[n_out:n_out + n_up], blocks[n_out + n_up:]
    stage_model, stage_ff, stage_sem = rest[10 + 2 * n_out + n_up:]
    tm = TILE_M
    n_chunks = tm // CHUNK

    def tile_step(i, carry):
        step = pl.program_id(0) * PIPE_DEPTH + i
        rows_of_tile = pl.ds(pl.multiple_of(i * tm, tm), tm)

        @pl.when(step == 0)
        def _():
            s1buf[...] = jnp.zeros((tm, D_MODEL), F32)
            resbuf[...] = jnp.zeros((tm, D_MODEL), F32)
            ubuf[...] = jnp.zeros((tm, D_MAIN), F32)
            lgbuf[...] = jnp.zeros((tm, D_QK), F32)
            _load_bf16_col_blocks(w1_hbm, w1_refs, stage_ff, stage_sem)
            _load_bf16_col_blocks(w2_hbm, w2_refs, stage_model, stage_sem)
            _load_bf16_col_blocks(wout_hbm, wout_refs, stage_model, stage_sem)

        @pl.when(jnp.clip(step - 1, 0, n_tiles - 1) % tiles_per_seq == 0)
        def _():
            hbuf[0:HALO - N_META, :] = jnp.zeros((HALO - N_META, D_CONV), F32)
            hbuf[HALO - N_META:HALO, :] = hmeta_ref[...]
            sbuf[...] = smeta_ref[...]

        s1_prev = s1buf[...]
        s1b = s1_prev.astype(BF16)
        u = ubuf[...]
        lg = lgbuf[...]
        s = resbuf[...]
        q = u[:, 2 * D_CONV:2 * D_CONV + D_QK]
        k = u[:, 2 * D_CONV + D_QK:2 * D_CONV + 2 * D_QK]
        v = u[:, 2 * D_CONV + 2 * D_QK:2 * D_CONV + 2 * D_QK + D_V]
        r = u[:, 2 * D_CONV + 2 * D_QK + D_V:D_MAIN]

        def up_block(c):
            hb = jnp.maximum(_dot(s1b, w1_refs[c][...]), 0.0)
            return (hb * hb).astype(BF16)

        b = _chunk_cumsum(lg, tm)
        total_cols = _chunk_total_cols(lg, tm)
        hid2 = [up_block(0), up_block(1)]

        s_new = _layer_norm(x_ref[rows_of_tile, :], lng_ref[...], lnb_ref[...])
        sb_new = s_new.astype(BF16)
        lg_new = _gate_projection(sb_new, wgd_ref, gup_ref, gbias_ref)

        qe = q * (GLA_DK ** -0.5) * jnp.exp(b)
        ke = (k * jnp.exp(-b)).astype(BF16)
        vb = v.astype(BF16)
        ri = lax.broadcasted_iota(jnp.int32, (tm, tm), 0)
        ci = lax.broadcasted_iota(jnp.int32, (tm, tm), 1)
        causal = (ri // CHUNK == ci // CHUNK) & (ci <= ri)
        lane_head = lax.broadcasted_iota(jnp.int32, (1, D_QK), 1) // GLA_DK
        qe_heads = [jnp.where(lane_head == h, qe, 0.0).astype(BF16) for h in range(GLA_HEADS)]
        scores = [_dot_t1(qe_heads[h], ke) for h in range(GLA_HEADS)]
        deltas = []
        for c in range(n_chunks):
            rows = slice(c * CHUNK, (c + 1) * CHUNK)
            b_c = b[rows, :]
            kd = k[rows, :] * jnp.exp(b_c[CHUNK - 1:CHUNK, :] - b_c)
            deltas.append(_state_delta(kd, v[rows, :]))
        hid2 += [up_block(2), up_block(3)]

        zero_v = jnp.zeros((tm, GLA_DV), BF16)
        o_intra = []
        for h in range(0, GLA_HEADS, 2):
            a_pair = jnp.concatenate(
                [jnp.where(causal, scores[h + d], 0.0).astype(BF16) for d in range(2)], axis=1)
            v0, v1 = (vb[:, (h + d) * GLA_DV:(h + d + 1) * GLA_DV] for d in range(2))
            v_pair = jnp.concatenate([jnp.concatenate([v0, zero_v], axis=1),
                                      jnp.concatenate([zero_v, v1], axis=1)], axis=0)
            o_intra.append(_dot(a_pair, v_pair))
        states = []
        state = sbuf[...]
        for c in range(n_chunks):
            states.append(state.astype(BF16))
            state = jnp.exp(total_cols[:, c * LANES:(c + 1) * LANES]) * state + deltas[c]
        hid2 += [up_block(4), up_block(5)]

        qe_b = qe.astype(BF16)
        zero_s = jnp.zeros((GLA_DK, GLA_DV), BF16)
        o_inter = []
        for c in range(n_chunks):
            s_bd = jnp.concatenate(
                [jnp.concatenate([states[c][h * GLA_DK:(h + 1) * GLA_DK, :] if g == h else zero_s
                                  for g in range(GLA_HEADS)], axis=1) for h in range(GLA_HEADS)], axis=0)
            o_inter.append(_dot(qe_b[c * CHUNK:(c + 1) * CHUNK, :], s_bd))
        hid2 += [up_block(6), up_block(7)]

        hid2 = jnp.concatenate(hid2, axis=1)
        f = jnp.concatenate([_dot(hid2, w[...]) for w in w2_refs], axis=1)

        hbuf[HALO:HALO + tm, :] = u[:, :D_CONV] * _sigmoid(u[:, D_CONV:2 * D_CONV])
        conv = _causal_conv(hbuf, cw_ref, tm) + cb_ref[...]
        halo = hbuf[tm:tm + HALO, :]
        conv_out = _silu(_layer_norm(conv, clg_ref[...], clb_ref[...]))
        o_all = jnp.concatenate(o_intra, axis=1) + jnp.concatenate(o_inter, axis=0)
        gla_blocks = []
        for h in range(GLA_HEADS):
            o = o_all[:, h * GLA_DV:(h + 1) * GLA_DV]
            o = o * lax.rsqrt(jnp.mean(o * o, axis=-1, keepdims=True) + LN_EPS) * gng_ref[...]
            gla_blocks.append(o * _silu(r[:, h * GLA_DV:(h + 1) * GLA_DV]))

        mixin = jnp.concatenate([conv_out] + gla_blocks, axis=-1).astype(BF16)
        mix = jnp.concatenate([_dot(mixin, w[...]) for w in wout_refs], axis=1)
        s1_new = _layer_norm(DEEPNORM_ALPHA * s + mix, l1g_ref[...], l1b_ref[...])

        out = _layer_norm(DEEPNORM_ALPHA * s1_prev + f, l2g_ref[...], l2b_ref[...])

        u_new = _dot(sb_new, wmain_ref[...])
        hbuf[0:HALO, :] = halo
        sbuf[...] = state
        s1buf[...] = s1_new
        o_ref[rows_of_tile, :] = out
        resbuf[...] = s_new
        ubuf[...] = u_new
        lgbuf[...] = lg_new
        return carry

    lax.fori_loop(0, PIPE_DEPTH, tile_step, 0)


def _resident(shape):
    return pl.BlockSpec(shape, lambda *_: (0,) * len(shape), pipeline_mode=pl.Buffered(1))


def kernel(x, meta_tokens, ln_in_g, ln_in_b, w_in, conv_w, conv_b, conv_ln_g, conv_ln_b,
           gate_up, gate_bias, gla_norm_g, w_out, ln1_g, ln1_b, w_ff1, w_ff2, ln2_g, ln2_b):
    batch, seq, d_model = x.shape
    assert d_model == D_MODEL and seq % (PIPE_DEPTH * TILE_M) == 0 and w_in.shape[0] == DEPTH
    row = lambda a: a.reshape(1, -1).astype(F32)

    w_main = w_in[0, :, :D_MAIN].astype(BF16)
    w_gd = jnp.pad(w_in[0, :, D_MAIN:], ((0, 0), (0, LANES - GLA_GATE_RANK))).astype(BF16)
    gup = jnp.pad(gate_up[0], ((0, LANES - GLA_GATE_RANK), (0, 0))).astype(BF16)
    lng, lnb, gbias = row(ln_in_g), row(ln_in_b), row(gate_bias[0])

    conv_taps = jnp.broadcast_to(conv_w[0].astype(F32)[:, None, :], (CONV_WIDTH, SUBLANES, D_CONV))

    proj_args = (lng, lnb, w_main, w_gd, gup, gbias)
    h_meta, s_meta = pl.pallas_call(
        _meta_kernel,
        out_shape=(jax.ShapeDtypeStruct((N_META, D_CONV), F32),
                   jax.ShapeDtypeStruct((D_QK, GLA_DV), F32)),
        compiler_params=pltpu.CompilerParams(vmem_limit_bytes=VMEM_LIMIT_BYTES),
        name="meta_state",
    )(meta_tokens.astype(F32), *proj_args)

    layer_args = (
        h_meta, s_meta, *proj_args,
        conv_taps, row(conv_b[0]), row(conv_ln_g[0]), row(conv_ln_b[0]),
        row(gla_norm_g[0]), row(ln1_g[0]), row(ln1_b[0]), row(ln2_g[0]), row(ln2_b[0]),
    )
    big = (w_out[0].astype(F32), w_ff1[0].astype(F32), w_ff2[0].astype(F32))
    big_specs = [pl.BlockSpec(memory_space=pl.ANY) for _ in big]
    big_blocks = [pltpu.VMEM((w.shape[0], W_COLS), BF16) for w in big for _ in range(w.shape[1] // W_COLS)]
    tiles_per_seq = seq // TILE_M
    n_tiles = batch * tiles_per_seq

    blocks_per_seq = tiles_per_seq // PIPE_DEPTH
    n_blocks = batch * blocks_per_seq

    def in_block(g):
        j = jnp.minimum(g, n_blocks - 1)
        return (j // blocks_per_seq, j % blocks_per_seq, 0)

    def out_block(g):
        j = jnp.maximum(g - 1, 0)
        return (j // blocks_per_seq, j % blocks_per_seq, 0)

    return pl.pallas_call(
        functools.partial(_layer_kernel, tiles_per_seq=tiles_per_seq, n_tiles=n_tiles),
        grid=(n_blocks + 1,),
        in_specs=[pl.BlockSpec((None, PIPE_DEPTH * TILE_M, D_MODEL), in_block)]
        + [_resident(a.shape) for a in layer_args] + big_specs,
        out_specs=pl.BlockSpec((None, PIPE_DEPTH * TILE_M, D_MODEL), out_block),
        out_shape=jax.ShapeDtypeStruct(x.shape, x.dtype),
        scratch_shapes=[pltpu.VMEM((HALO + TILE_M, D_CONV), F32),
                        pltpu.VMEM((D_QK, GLA_DV), F32),
                        pltpu.VMEM((TILE_M, D_MODEL), F32),
                        pltpu.VMEM((TILE_M, D_MAIN), F32),
                        pltpu.VMEM((TILE_M, D_QK), F32),
                        pltpu.VMEM((TILE_M, D_MODEL), F32),
                        *big_blocks,
                        pltpu.VMEM((2, STAGE_ROWS, D_MODEL), F32),
                        pltpu.VMEM((2, STAGE_ROWS, D_FF), F32),
                        pltpu.SemaphoreType.DMA((2,))],
        compiler_params=pltpu.CompilerParams(
            dimension_semantics=("arbitrary",),
            vmem_limit_bytes=VMEM_LIMIT_BYTES),
        name="layer",
    )(x, *layer_args, *big)
```

```python
import functools

import jax
import jax.numpy as jnp
from jax import lax
from jax.experimental import pallas as pl
from jax.experimental.pallas import tpu as pltpu

D_MODEL = 1024
N_META = 16
D_CONV = 512
CONV_WIDTH = 31
GLA_HEADS = 4
GLA_DV = 128
GLA_DK = 64
GLA_GATE_RANK = 16
GLA_TAU = 16.0
CHUNK = 64
D_FF = 4096
LN_EPS = 1e-5
DEPTH = 1
DEEPNORM_ALPHA = (2.0 * DEPTH) ** 0.25

D_QK = GLA_HEADS * GLA_DK
D_V = GLA_HEADS * GLA_DV
D_MAIN = 2 * D_CONV + 2 * D_QK + 2 * D_V
LANES = 128
SUBLANES = 8
HALO = 32
CONV_HOP_EVERY = 3
W_COLS = 512
STAGE_ROWS = 128
STAGE_SLOTS = 4
PIPE_DEPTH = 2
TILE_M = 256
VMEM_LIMIT_BYTES = 56 * 1024 * 1024

F32 = jnp.float32
BF16 = jnp.bfloat16


def _dot(a, b):
    return jnp.dot(a, b, preferred_element_type=F32)


def _dot_t0(a, b):
    return lax.dot_general(a, b, (((0,), (0,)), ((), ())), preferred_element_type=F32)


def _dot_t1(a, b):
    return lax.dot_general(a, b, (((1,), (1,)), ((), ())), preferred_element_type=F32)


def _layer_norm(x, g, b):
    mu = jnp.mean(x, axis=-1, keepdims=True)
    xc = x - mu
    var = jnp.mean(xc * xc, axis=-1, keepdims=True)
    return xc * lax.rsqrt(var + LN_EPS) * g + b


def _sigmoid(x):
    return 1.0 / (1.0 + jnp.exp(-x))


def _silu(x):
    return x * _sigmoid(x)


def _log_sigmoid(z):
    return jnp.minimum(z, 0.0) - jnp.log(1.0 + jnp.exp(-jnp.abs(z)))


def _zero_after(x):
    bits = lax.bitcast_convert_type(x, jnp.uint32)
    half = jnp.uint32(16)
    zero_bits = lax.shift_right_logical(lax.shift_right_logical(bits, half), half)
    return lax.bitcast_convert_type(zero_bits, F32)


def _split_hi_lo(x):
    hi = x.astype(BF16)
    lo = (x - hi.astype(F32)).astype(BF16)
    return hi, lo


def _chunk_cumsum(lg, rows):
    ri = lax.broadcasted_iota(jnp.int32, (rows, rows), 0)
    ci = lax.broadcasted_iota(jnp.int32, (rows, rows), 1)
    tri = jnp.where((ri // CHUNK == ci // CHUNK) & (ci <= ri), 1.0, 0.0).astype(BF16)
    hi, lo = _split_hi_lo(lg)
    return _dot(tri, hi) + _dot(tri, lo)


def _chunk_total_cols(lg, rows):
    n_chunks = rows // CHUNK
    ri = lax.broadcasted_iota(jnp.int32, (rows, n_chunks * LANES), 0)
    ci = lax.broadcasted_iota(jnp.int32, (rows, n_chunks * LANES), 1)
    ind = jnp.where(ri // CHUNK == ci // LANES, 1.0, 0.0).astype(BF16)
    hi, lo = _split_hi_lo(lg)
    return _dot_t0(hi, ind) + _dot_t0(lo, ind)


def _state_delta(kd_c, v_c):
    full = _dot_t0(kd_c.astype(BF16), v_c.astype(BF16))
    return jnp.concatenate(
        [full[h * GLA_DK:(h + 1) * GLA_DK, h * GLA_DV:(h + 1) * GLA_DV] for h in range(GLA_HEADS)],
        axis=0)


def _causal_conv(hbuf, cw_ref, n_rows):
    s0 = HALO - (CONV_WIDTH - 1)
    n_groups = n_rows // SUBLANES
    sub = lax.broadcasted_iota(jnp.int32, (SUBLANES, LANES), 0)
    n_base = (CONV_WIDTH + s0 - 2) // SUBLANES + 1

    def taps(phase):
        return [(a, SUBLANES * a + phase - s0) for a in range(n_base + 1)
                if 0 <= SUBLANES * a + phase - s0 < CONV_WIDTH]

    def piece(p, lanes):
        return hbuf[p * SUBLANES:(p + 1) * SUBLANES, lanes]

    def phase_sum(phase, pieces, first):
        acc = None
        for a, tap in taps(phase):
            term = pieces[a - first] * cw_ref[tap, :, lanes_of[0]]
            acc = term if acc is None else acc + term
        return acc

    strips = []
    chain = None
    lanes_of = [None]
    for lt in range(D_CONV // LANES):
        lanes = slice(lt * LANES, (lt + 1) * LANES)
        lanes_of[0] = lanes
        first_pieces = [piece(a, lanes) for a in range(n_base)]
        rolled = {r: pltpu.roll(phase_sum(r, first_pieces, 0), SUBLANES - r, 0)
                  for r in range(1, SUBLANES)}
        outs = []
        for i in range(n_groups):
            pieces = [piece(i + 1 + a, lanes) for a in range(n_base)]
            if chain is not None:
                pieces[-1] = pieces[-1] + chain
            out = phase_sum(0, pieces, 1)
            for r in range(1, SUBLANES):
                nxt = pltpu.roll(phase_sum(r, pieces, 0), SUBLANES - r, 0)
                out = out + jnp.where(sub < SUBLANES - r, rolled[r], nxt)
                rolled[r] = nxt
            outs.append(out)
            chain = _zero_after(out)
            if (lt * n_groups + i) % CONV_HOP_EVERY == 0:
                chain = pltpu.roll(chain, 1, 1)
        strips.append(jnp.concatenate(outs, axis=0))
    return jnp.concatenate(strips, axis=1)


def _gate_projection(sb, wgd_ref, gup_ref, gbias_ref):
    gd = _dot(sb, wgd_ref[...])
    z = _dot(gd.astype(BF16), gup_ref[...]) + gbias_ref[...]
    return _log_sigmoid(z) * (1.0 / GLA_TAU)


def _in_projection(s, wmain_ref, wgd_ref, gup_ref, gbias_ref):
    sb = s.astype(BF16)
    u = _dot(sb, wmain_ref[...])
    return u, _gate_projection(sb, wgd_ref, gup_ref, gbias_ref)


def _meta_kernel(meta_ref, lng_ref, lnb_ref, wmain_ref, wgd_ref, gup_ref, gbias_ref,
                 hmeta_ref, smeta_ref):
    pad = CHUNK - N_META
    s = _layer_norm(meta_ref[...], lng_ref[...], lnb_ref[...])
    s = jnp.concatenate([jnp.zeros((pad, D_MODEL), F32), s], axis=0)
    u, lg = _in_projection(s, wmain_ref, wgd_ref, gup_ref, gbias_ref)
    h = u[:, :D_CONV] * _sigmoid(u[:, D_CONV:2 * D_CONV])
    hmeta_ref[...] = h[pad:, :]
    is_meta = lax.broadcasted_iota(jnp.int32, (CHUNK, 1), 0) >= pad
    k = jnp.where(is_meta, u[:, 2 * D_CONV + D_QK:2 * D_CONV + 2 * D_QK], 0.0)
    v = jnp.where(is_meta, u[:, 2 * D_CONV + 2 * D_QK:2 * D_CONV + 2 * D_QK + D_V], 0.0)
    lg = jnp.where(is_meta, lg, 0.0)
    b = _chunk_cumsum(lg, CHUNK)
    kd = k * jnp.exp(b[CHUNK - 1:CHUNK, :] - b)
    smeta_ref[...] = _state_delta(kd, v)


def _load_bf16_col_blocks(w_hbm, blocks, stage, sem):
    n_chunks = w_hbm.shape[0] // STAGE_ROWS

    def chunk_copy(c, slot):
        return pltpu.make_async_copy(w_hbm.at[pl.ds(c * STAGE_ROWS, STAGE_ROWS), :],
                                     stage.at[slot], sem.at[slot])

    for c in range(STAGE_SLOTS - 1):
        chunk_copy(c, c).start()

    def convert_chunk(c, carry):
        slot = c % STAGE_SLOTS
        ahead = c + STAGE_SLOTS - 1

        @pl.when(ahead < n_chunks)
        def _():
            chunk_copy(ahead, ahead % STAGE_SLOTS).start()

        chunk_copy(c, slot).wait()
        rows = pl.ds(pl.multiple_of(c * STAGE_ROWS, STAGE_ROWS), STAGE_ROWS)
        for b, blk in enumerate(blocks):
            blk[rows, :] = stage[slot, :, b * W_COLS:(b + 1) * W_COLS].astype(BF16)
        return carry

    lax.fori_loop(0, n_chunks, convert_chunk, 0)


def _layer_kernel(x_ref, hmeta_ref, smeta_ref, lng_ref, lnb_ref, wmain_ref, wgd_ref, gup_ref,
                  gbias_ref, cw_ref, cb_ref, clg_ref, clb_ref, gng_ref, l1g_ref, l1b_ref,
                  l2g_ref, l2b_ref, *rest, tiles_per_seq, n_tiles):
    n_out, n_up = D_MODEL // W_COLS, D_FF // W_COLS
    wout_hbm, w1_hbm, w2_hbm, o_ref, hbuf, sbuf, s1buf, ubuf, lgbuf, resbuf = rest[:10]
    blocks = rest[10:10 + 2 * n_out + n_up]
    wout_refs, w1_refs, w2_refs = blocks[:n_out], blocks[n_out:n_out + n_up], blocks[n_out + n_up:]
    stage_model, stage_ff, stage_sem = rest[10 + 2 * n_out + n_up:]
    tm = TILE_M
    n_chunks = tm // CHUNK

    def tile_step(i, carry):
        step = pl.program_id(0) * PIPE_DEPTH + i
        rows_of_tile = pl.ds(pl.multiple_of(i * tm, tm), tm)

        @pl.when(step == 0)
        def _():
            s1buf[...] = jnp.zeros((tm, D_MODEL), F32)
            resbuf[...] = jnp.zeros((tm, D_MODEL), F32)
            ubuf[...] = jnp.zeros((tm, D_MAIN), F32)
            lgbuf[...] = jnp.zeros((tm, D_QK), F32)
            _load_bf16_col_blocks(w1_hbm, w1_refs, stage_ff, stage_sem)
            _load_bf16_col_blocks(w2_hbm, w2_refs, stage_model, stage_sem)
            _load_bf16_col_blocks(wout_hbm, wout_refs, stage_model, stage_sem)

        @pl.when(jnp.clip(step - 1, 0, n_tiles - 1) % tiles_per_seq == 0)
        def _():
            hbuf[0:HALO - N_META, :] = jnp.zeros((HALO - N_META, D_CONV), F32)
            hbuf[HALO - N_META:HALO, :] = hmeta_ref[...]
            sbuf[...] = smeta_ref[...]

        s1_prev = s1buf[...]
        s1b = s1_prev.astype(BF16)
        u = ubuf[...]
        lg = lgbuf[...]
        s = resbuf[...]
        q = u[:, 2 * D_CONV:2 * D_CONV + D_QK]
        k = u[:, 2 * D_CONV + D_QK:2 * D_CONV + 2 * D_QK]
        v = u[:, 2 * D_CONV + 2 * D_QK:2 * D_CONV + 2 * D_QK + D_V]
        r = u[:, 2 * D_CONV + 2 * D_QK + D_V:D_MAIN]

        def up_block(c):
            hb = jnp.maximum(_dot(s1b, w1_refs[c][...]), 0.0)
            return (hb * hb).astype(BF16)

        b = _chunk_cumsum(lg, tm)
        total_cols = _chunk_total_cols(lg, tm)
        hid2 = [up_block(0), up_block(1)]

        s_new = _layer_norm(x_ref[rows_of_tile, :], lng_ref[...], lnb_ref[...])
        sb_new = s_new.astype(BF16)
        lg_new = _gate_projection(sb_new, wgd_ref, gup_ref, gbias_ref)

        qe = q * (GLA_DK ** -0.5) * jnp.exp(b)
        ke = (k * jnp.exp(-b)).astype(BF16)
        vb = v.astype(BF16)
        ri = lax.broadcasted_iota(jnp.int32, (tm, tm), 0)
        ci = lax.broadcasted_iota(jnp.int32, (tm, tm), 1)
        causal = (ri // CHUNK == ci // CHUNK) & (ci <= ri)
        lane_head = lax.broadcasted_iota(jnp.int32, (1, D_QK), 1) // GLA_DK
        qe_heads = [jnp.where(lane_head == h, qe, 0.0).astype(BF16) for h in range(GLA_HEADS)]
        scores = [_dot_t1(qe_heads[h], ke) for h in range(GLA_HEADS)]
        deltas = []
        for c in range(n_chunks):
            rows = slice(c * CHUNK, (c + 1) * CHUNK)
            b_c = b[rows, :]
            kd = k[rows, :] * jnp.exp(b_c[CHUNK - 1:CHUNK, :] - b_c)
            deltas.append(_state_delta(kd, v[rows, :]))
        hid2 += [up_block(2), up_block(3)]

        zero_v = jnp.zeros((tm, GLA_DV), BF16)
        o_intra = []
        for h in range(0, GLA_HEADS, 2):
            a_pair = jnp.concatenate(
                [jnp.where(causal, scores[h + d], 0.0).astype(BF16) for d in range(2)], axis=1)
            v0, v1 = (vb[:, (h + d) * GLA_DV:(h + d + 1) * GLA_DV] for d in range(2))
            v_pair = jnp.concatenate([jnp.concatenate([v0, zero_v], axis=1),
                                      jnp.concatenate([zero_v, v1], axis=1)], axis=0)
            o_intra.append(_dot(a_pair, v_pair))
        states = []
        state = sbuf[...]
        for c in range(n_chunks):
            states.append(state.astype(BF16))
            state = jnp.exp(total_cols[:, c * LANES:(c + 1) * LANES]) * state + deltas[c]
        hid2 += [up_block(4), up_block(5)]

        qe_b = qe.astype(BF16)
        zero_s = jnp.zeros((GLA_DK, GLA_DV), BF16)
        o_inter = []
        for c in range(n_chunks):
            s_bd = jnp.concatenate(
                [jnp.concatenate([states[c][h * GLA_DK:(h + 1) * GLA_DK, :] if g == h else zero_s
                                  for g in range(GLA_HEADS)], axis=1) for h in range(GLA_HEADS)], axis=0)
            o_inter.append(_dot(qe_b[c * CHUNK:(c + 1) * CHUNK, :], s_bd))
        hid2 += [up_block(6), up_block(7)]

        hid2 = jnp.concatenate(hid2, axis=1)
        f = jnp.concatenate([_dot(hid2, w[...]) for w in w2_refs], axis=1)

        hbuf[HALO:HALO + tm, :] = u[:, :D_CONV] * _sigmoid(u[:, D_CONV:2 * D_CONV])
        conv = _causal_conv(hbuf, cw_ref, tm) + cb_ref[...]
        halo = hbuf[tm:tm + HALO, :]
        conv_out = _silu(_layer_norm(conv, clg_ref[...], clb_ref[...]))
        o_all = jnp.concatenate(o_intra, axis=1) + jnp.concatenate(o_inter, axis=0)
        gla_blocks = []
        for h in range(GLA_HEADS):
            o = o_all[:, h * GLA_DV:(h + 1) * GLA_DV]
            o = o * lax.rsqrt(jnp.mean(o * o, axis=-1, keepdims=True) + LN_EPS) * gng_ref[...]
            gla_blocks.append(o * _silu(r[:, h * GLA_DV:(h + 1) * GLA_DV]))

        mixin = jnp.concatenate([conv_out] + gla_blocks, axis=-1).astype(BF16)
        mix = jnp.concatenate([_dot(mixin, w[...]) for w in wout_refs], axis=1)
        s1_new = _layer_norm(DEEPNORM_ALPHA * s + mix, l1g_ref[...], l1b_ref[...])

        out = _layer_norm(DEEPNORM_ALPHA * s1_prev + f, l2g_ref[...], l2b_ref[...])

        u_new = _dot(sb_new, wmain_ref[...])
        hbuf[0:HALO, :] = halo
        sbuf[...] = state
        s1buf[...] = s1_new
        o_ref[rows_of_tile, :] = out
        resbuf[...] = s_new
        ubuf[...] = u_new
        lgbuf[...] = lg_new
        return carry

    lax.fori_loop(0, PIPE_DEPTH, tile_step, 0)


def _resident(shape):
    return pl.BlockSpec(shape, lambda *_: (0,) * len(shape), pipeline_mode=pl.Buffered(1))


def kernel(x, meta_tokens, ln_in_g, ln_in_b, w_in, conv_w, conv_b, conv_ln_g, conv_ln_b,
           gate_up, gate_bias, gla_norm_g, w_out, ln1_g, ln1_b, w_ff1, w_ff2, ln2_g, ln2_b):
    batch, seq, d_model = x.shape
    assert d_model == D_MODEL and seq % (PIPE_DEPTH * TILE_M) == 0 and w_in.shape[0] == DEPTH
    row = lambda a: a.reshape(1, -1).astype(F32)

    w_main = w_in[0, :, :D_MAIN].astype(BF16)
    w_gd = jnp.pad(w_in[0, :, D_MAIN:], ((0, 0), (0, LANES - GLA_GATE_RANK))).astype(BF16)
    gup = jnp.pad(gate_up[0], ((0, LANES - GLA_GATE_RANK), (0, 0))).astype(BF16)
    lng, lnb, gbias = row(ln_in_g), row(ln_in_b), row(gate_bias[0])

    conv_taps = jnp.broadcast_to(conv_w[0].astype(F32)[:, None, :], (CONV_WIDTH, SUBLANES, D_CONV))

    proj_args = (lng, lnb, w_main, w_gd, gup, gbias)
    h_meta, s_meta = pl.pallas_call(
        _meta_kernel,
        out_shape=(jax.ShapeDtypeStruct((N_META, D_CONV), F32),
                   jax.ShapeDtypeStruct((D_QK, GLA_DV), F32)),
        compiler_params=pltpu.CompilerParams(vmem_limit_bytes=VMEM_LIMIT_BYTES),
        name="meta_state",
    )(meta_tokens.astype(F32), *proj_args)

    layer_args = (
        h_meta, s_meta, *proj_args,
        conv_taps, row(conv_b[0]), row(conv_ln_g[0]), row(conv_ln_b[0]),
        row(gla_norm_g[0]), row(ln1_g[0]), row(ln1_b[0]), row(ln2_g[0]), row(ln2_b[0]),
    )
    big = (w_out[0].astype(F32), w_ff1[0].astype(F32), w_ff2[0].astype(F32))
    big_specs = [pl.BlockSpec(memory_space=pl.ANY) for _ in big]
    big_blocks = [pltpu.VMEM((w.shape[0], W_COLS), BF16) for w in big for _ in range(w.shape[1] // W_COLS)]
    tiles_per_seq = seq // TILE_M
    n_tiles = batch * tiles_per_seq

    blocks_per_seq = tiles_per_seq // PIPE_DEPTH
    n_blocks = batch * blocks_per_seq

    def in_block(g):
        j = jnp.minimum(g, n_blocks - 1)
        return (j // blocks_per_seq, j % blocks_per_seq, 0)

    def out_block(g):
        j = jnp.maximum(g - 1, 0)
        return (j // blocks_per_seq, j % blocks_per_seq, 0)

    return pl.pallas_call(
        functools.partial(_layer_kernel, tiles_per_seq=tiles_per_seq, n_tiles=n_tiles),
        grid=(n_blocks + 1,),
        in_specs=[pl.BlockSpec((None, PIPE_DEPTH * TILE_M, D_MODEL), in_block)]
        + [_resident(a.shape) for a in layer_args] + big_specs,
        out_specs=pl.BlockSpec((None, PIPE_DEPTH * TILE_M, D_MODEL), out_block),
        out_shape=jax.ShapeDtypeStruct(x.shape, x.dtype),
        scratch_shapes=[pltpu.VMEM((HALO + TILE_M, D_CONV), F32),
                        pltpu.VMEM((D_QK, GLA_DV), F32),
                        pltpu.VMEM((TILE_M, D_MODEL), F32),
                        pltpu.VMEM((TILE_M, D_MAIN), F32),
                        pltpu.VMEM((TILE_M, D_QK), F32),
                        pltpu.VMEM((TILE_M, D_MODEL), F32),
                        *big_blocks,
                        pltpu.VMEM((STAGE_SLOTS, STAGE_ROWS, D_MODEL), F32),
                        pltpu.VMEM((STAGE_SLOTS, STAGE_ROWS, D_FF), F32),
                        pltpu.SemaphoreType.DMA((STAGE_SLOTS,))],
        compiler_params=pltpu.CompilerParams(
            dimension_semantics=("arbitrary",),
            vmem_limit_bytes=VMEM_LIMIT_BYTES),
        name="layer",
    )(x, *layer_args, *big)
```

```python
import functools

import jax
import jax.numpy as jnp
from jax import lax
from jax.experimental import pallas as pl
from jax.experimental.pallas import tpu as pltpu

D_MODEL = 1024
N_META = 16
D_CONV = 512
CONV_WIDTH = 31
GLA_HEADS = 4
GLA_DV = 128
GLA_DK = 64
GLA_GATE_RANK = 16
GLA_TAU = 16.0
CHUNK = 64
D_FF = 4096
LN_EPS = 1e-5
DEPTH = 1
DEEPNORM_ALPHA = (2.0 * DEPTH) ** 0.25

D_QK = GLA_HEADS * GLA_DK
D_V = GLA_HEADS * GLA_DV
D_MAIN = 2 * D_CONV + 2 * D_QK + 2 * D_V
LANES = 128
SUBLANES = 8
HALO = 32
CONV_HOP_EVERY = 3
W_COLS = 512
STAGE_ROWS = 128
STAGE_SLOTS = 4
PIPE_DEPTH = 2
TILE_M = 256
VMEM_LIMIT_BYTES = 56 * 1024 * 1024

F32 = jnp.float32
BF16 = jnp.bfloat16


def _dot(a, b):
    return jnp.dot(a, b, preferred_element_type=F32)


def _dot_t0(a, b):
    return lax.dot_general(a, b, (((0,), (0,)), ((), ())), preferred_element_type=F32)


def _dot_t1(a, b):
    return lax.dot_general(a, b, (((1,), (1,)), ((), ())), preferred_element_type=F32)


def _layer_norm(x, g, b):
    mu = jnp.mean(x, axis=-1, keepdims=True)
    xc = x - mu
    var = jnp.mean(xc * xc, axis=-1, keepdims=True)
    return xc * lax.rsqrt(var + LN_EPS) * g + b


def _sigmoid(x):
    return 1.0 / (1.0 + jnp.exp(-x))


def _silu(x):
    return x * _sigmoid(x)


def _log_sigmoid(z):
    return jnp.minimum(z, 0.0) - jnp.log(1.0 + jnp.exp(-jnp.abs(z)))


def _zero_after(x):
    bits = lax.bitcast_convert_type(x, jnp.uint32)
    half = jnp.uint32(16)
    zero_bits = lax.shift_right_logical(lax.shift_right_logical(bits, half), half)
    return lax.bitcast_convert_type(zero_bits, F32)


def _split_hi_lo(x):
    hi = x.astype(BF16)
    lo = (x - hi.astype(F32)).astype(BF16)
    return hi, lo


def _chunk_cumsum(lg, rows):
    ri = lax.broadcasted_iota(jnp.int32, (rows, rows), 0)
    ci = lax.broadcasted_iota(jnp.int32, (rows, rows), 1)
    tri = jnp.where((ri // CHUNK == ci // CHUNK) & (ci <= ri), 1.0, 0.0).astype(BF16)
    hi, lo = _split_hi_lo(lg)
    return _dot(tri, hi) + _dot(tri, lo)


def _chunk_total_cols(lg, rows):
    n_chunks = rows // CHUNK
    ri = lax.broadcasted_iota(jnp.int32, (rows, n_chunks * LANES), 0)
    ci = lax.broadcasted_iota(jnp.int32, (rows, n_chunks * LANES), 1)
    ind = jnp.where(ri // CHUNK == ci // LANES, 1.0, 0.0).astype(BF16)
    hi, lo = _split_hi_lo(lg)
    return _dot_t0(hi, ind) + _dot_t0(lo, ind)


def _state_delta(kd_c, v_c):
    full = _dot_t0(kd_c.astype(BF16), v_c.astype(BF16))
    return jnp.concatenate(
        [full[h * GLA_DK:(h + 1) * GLA_DK, h * GLA_DV:(h + 1) * GLA_DV] for h in range(GLA_HEADS)],
        axis=0)


def _causal_conv(hbuf, cw_ref, n_rows):
    s0 = HALO - (CONV_WIDTH - 1)
    n_groups = n_rows // SUBLANES
    sub = lax.broadcasted_iota(jnp.int32, (SUBLANES, LANES), 0)
    n_base = (CONV_WIDTH + s0 - 2) // SUBLANES + 1

    def taps(phase):
        return [(a, SUBLANES * a + phase - s0) for a in range(n_base + 1)
                if 0 <= SUBLANES * a + phase - s0 < CONV_WIDTH]

    def piece(p, lanes):
        return hbuf[p * SUBLANES:(p + 1) * SUBLANES, lanes]

    def phase_sum(phase, pieces, first):
        acc = None
        for a, tap in taps(phase):
            term = pieces[a - first] * cw_ref[tap, :, lanes_of[0]]
            acc = term if acc is None else acc + term
        return acc

    strips = []
    chain = None
    lanes_of = [None]
    for lt in range(D_CONV // LANES):
        lanes = slice(lt * LANES, (lt + 1) * LANES)
        lanes_of[0] = lanes
        first_pieces = [piece(a, lanes) for a in range(n_base)]
        rolled = {r: pltpu.roll(phase_sum(r, first_pieces, 0), SUBLANES - r, 0)
                  for r in range(1, SUBLANES)}
        outs = []
        for i in range(n_groups):
            pieces = [piece(i + 1 + a, lanes) for a in range(n_base)]
            if chain is not None:
                pieces[-1] = pieces[-1] + chain
            out = phase_sum(0, pieces, 1)
            for r in range(1, SUBLANES):
                nxt = pltpu.roll(phase_sum(r, pieces, 0), SUBLANES - r, 0)
                out = out + jnp.where(sub < SUBLANES - r, rolled[r], nxt)
                rolled[r] = nxt
            outs.append(out)
            chain = _zero_after(out)
            if (lt * n_groups + i) % CONV_HOP_EVERY == 0:
                chain = pltpu.roll(chain, 1, 1)
        strips.append(jnp.concatenate(outs, axis=0))
    return jnp.concatenate(strips, axis=1)


def _gate_projection(sb, wgd_ref, gup_ref, gbias_ref):
    gd = _dot(sb, wgd_ref[...])
    z = _dot(gd.astype(BF16), gup_ref[...]) + gbias_ref[...]
    return _log_sigmoid(z) * (1.0 / GLA_TAU)


def _in_projection(s, wmain_ref, wgd_ref, gup_ref, gbias_ref):
    sb = s.astype(BF16)
    u = _dot(sb, wmain_ref[...])
    return u, _gate_projection(sb, wgd_ref, gup_ref, gbias_ref)


def _meta_kernel(meta_ref, lng_ref, lnb_ref, wmain_ref, wgd_ref, gup_ref, gbias_ref,
                 hmeta_ref, smeta_ref):
    pad = CHUNK - N_META
    s = _layer_norm(meta_ref[...], lng_ref[...], lnb_ref[...])
    s = jnp.concatenate([jnp.zeros((pad, D_MODEL), F32), s], axis=0)
    u, lg = _in_projection(s, wmain_ref, wgd_ref, gup_ref, gbias_ref)
    h = u[:, :D_CONV] * _sigmoid(u[:, D_CONV:2 * D_CONV])
    hmeta_ref[...] = h[pad:, :]
    is_meta = lax.broadcasted_iota(jnp.int32, (CHUNK, 1), 0) >= pad
    k = jnp.where(is_meta, u[:, 2 * D_CONV + D_QK:2 * D_CONV + 2 * D_QK], 0.0)
    v = jnp.where(is_meta, u[:, 2 * D_CONV + 2 * D_QK:2 * D_CONV + 2 * D_QK + D_V], 0.0)
    lg = jnp.where(is_meta, lg, 0.0)
    b = _chunk_cumsum(lg, CHUNK)
    kd = k * jnp.exp(b[CHUNK - 1:CHUNK, :] - b)
    smeta_ref[...] = _state_delta(kd, v)


def _load_bf16_col_blocks(w_hbm, blocks, stage, sem):
    n_chunks = w_hbm.shape[0] // STAGE_ROWS

    def chunk_copy(c, slot):
        return pltpu.make_async_copy(w_hbm.at[pl.ds(c * STAGE_ROWS, STAGE_ROWS), :],
                                     stage.at[slot], sem.at[slot])

    for c in range(STAGE_SLOTS - 1):
        chunk_copy(c, c).start()

    def convert_chunk(c, carry):
        slot = c % STAGE_SLOTS
        ahead = c + STAGE_SLOTS - 1

        @pl.when(ahead < n_chunks)
        def _():
            chunk_copy(ahead, ahead % STAGE_SLOTS).start()

        chunk_copy(c, slot).wait()
        rows = pl.ds(pl.multiple_of(c * STAGE_ROWS, STAGE_ROWS), STAGE_ROWS)
        for b, blk in enumerate(blocks):
            blk[rows, :] = stage[slot, :, b * W_COLS:(b + 1) * W_COLS].astype(BF16)
        return carry

    lax.fori_loop(0, n_chunks, convert_chunk, 0)


def _layer_kernel(x_ref, hmeta_ref, smeta_ref, lng_ref, lnb_ref, wmain_ref, wgd_ref, gup_ref,
                  gbias_ref, cw_ref, cb_ref, clg_ref, clb_ref, gng_ref, l1g_ref, l1b_ref,
                  l2g_ref, l2b_ref, *rest, tiles_per_seq, n_tiles):
    n_out, n_up = D_MODEL // W_COLS, D_FF // W_COLS
    wout_hbm, w1_hbm, w2_hbm, o_ref, hbuf, sbuf, s1buf, ubuf, lgbuf, resbuf, *blocks = rest
    *blocks, stage_model, stage_ff, stage_sem = blocks
    wout_refs, w1_refs, w2_refs = blocks[:n_out], blocks[n_out:n_out + n_up], blocks[n_out + n_up:]
    tm = TILE_M
    n_chunks = tm // CHUNK

    def tile_step(i, carry):
        step = pl.program_id(0) * PIPE_DEPTH + i
        rows_of_tile = pl.ds(pl.multiple_of(i * tm, tm), tm)

        @pl.when(step == 0)
        def _():
            s1buf[...] = jnp.zeros((tm, D_MODEL), F32)
            resbuf[...] = jnp.zeros((tm, D_MODEL), F32)
            ubuf[...] = jnp.zeros((tm, D_MAIN), F32)
            lgbuf[...] = jnp.zeros((tm, D_QK), F32)
            _load_bf16_col_blocks(w1_hbm, w1_refs, stage_ff, stage_sem)
            _load_bf16_col_blocks(w2_hbm, w2_refs, stage_model, stage_sem)
            _load_bf16_col_blocks(wout_hbm, wout_refs, stage_model, stage_sem)

        @pl.when(jnp.clip(step - 1, 0, n_tiles - 1) % tiles_per_seq == 0)
        def _():
            hbuf[0:HALO - N_META, :] = jnp.zeros((HALO - N_META, D_CONV), F32)
            hbuf[HALO - N_META:HALO, :] = hmeta_ref[...]
            sbuf[...] = smeta_ref[...]

        s1_prev = s1buf[...]
        s1b = s1_prev.astype(BF16)
        u = ubuf[...]
        lg = lgbuf[...]
        s = resbuf[...]
        q = u[:, 2 * D_CONV:2 * D_CONV + D_QK]
        k = u[:, 2 * D_CONV + D_QK:2 * D_CONV + 2 * D_QK]
        v = u[:, 2 * D_CONV + 2 * D_QK:2 * D_CONV + 2 * D_QK + D_V]
        r = u[:, 2 * D_CONV + 2 * D_QK + D_V:D_MAIN]

        def up_block(c):
            hb = jnp.maximum(_dot(s1b, w1_refs[c][...]), 0.0)
            return (hb * hb).astype(BF16)

        b = _chunk_cumsum(lg, tm)
        total_cols = _chunk_total_cols(lg, tm)
        hid2 = [up_block(0), up_block(1)]

        s_new = _layer_norm(x_ref[rows_of_tile, :], lng_ref[...], lnb_ref[...])
        sb_new = s_new.astype(BF16)
        lg_new = _gate_projection(sb_new, wgd_ref, gup_ref, gbias_ref)

        qe = q * (GLA_DK ** -0.5) * jnp.exp(b)
        ke = (k * jnp.exp(-b)).astype(BF16)
        vb = v.astype(BF16)
        ri = lax.broadcasted_iota(jnp.int32, (tm, tm), 0)
        ci = lax.broadcasted_iota(jnp.int32, (tm, tm), 1)
        causal = (ri // CHUNK == ci // CHUNK) & (ci <= ri)
        lane_head = lax.broadcasted_iota(jnp.int32, (1, D_QK), 1) // GLA_DK
        qe_heads = [jnp.where(lane_head == h, qe, 0.0).astype(BF16) for h in range(GLA_HEADS)]
        scores = [_dot_t1(qe_heads[h], ke) for h in range(GLA_HEADS)]
        deltas = []
        for c in range(n_chunks):
            rows = slice(c * CHUNK, (c + 1) * CHUNK)
            b_c = b[rows, :]
            kd = k[rows, :] * jnp.exp(b_c[CHUNK - 1:CHUNK, :] - b_c)
            deltas.append(_state_delta(kd, v[rows, :]))
        hid2 += [up_block(2), up_block(3)]

        zero_v = jnp.zeros((tm, GLA_DV), BF16)
        o_intra = []
        for h in range(0, GLA_HEADS, 2):
            a_pair = jnp.concatenate(
                [jnp.where(causal, scores[h + d], 0.0).astype(BF16) for d in range(2)], axis=1)
            v0, v1 = (vb[:, (h + d) * GLA_DV:(h + d + 1) * GLA_DV] for d in range(2))
            v_pair = jnp.concatenate([jnp.concatenate([v0, zero_v], axis=1),
                                      jnp.concatenate([zero_v, v1], axis=1)], axis=0)
            o_intra.append(_dot(a_pair, v_pair))
        states = []
        state = sbuf[...]
        for c in range(n_chunks):
            states.append(state.astype(BF16))
            state = jnp.exp(total_cols[:, c * LANES:(c + 1) * LANES]) * state + deltas[c]
        hid2 += [up_block(4), up_block(5)]

        qe_b = qe.astype(BF16)
        zero_s = jnp.zeros((GLA_DK, GLA_DV), BF16)
        o_inter = []
        for c in range(n_chunks):
            s_bd = jnp.concatenate(
                [jnp.concatenate([states[c][h * GLA_DK:(h + 1) * GLA_DK, :] if g == h else zero_s
                                  for g in range(GLA_HEADS)], axis=1) for h in range(GLA_HEADS)], axis=0)
            o_inter.append(_dot(qe_b[c * CHUNK:(c + 1) * CHUNK, :], s_bd))
        hid2 += [up_block(6), up_block(7)]

        hid2 = jnp.concatenate(hid2, axis=1)
        f = jnp.concatenate([_dot(hid2, w[...]) for w in w2_refs], axis=1)

        hbuf[HALO:HALO + tm, :] = u[:, :D_CONV] * _sigmoid(u[:, D_CONV:2 * D_CONV])
        conv = _causal_conv(hbuf, cw_ref, tm) + cb_ref[...]
        halo = hbuf[tm:tm + HALO, :]
        conv_out = _silu(_layer_norm(conv, clg_ref[...], clb_ref[...]))
        o_all = jnp.concatenate(o_intra, axis=1) + jnp.concatenate(o_inter, axis=0)
        gla_blocks = []
        for h in range(GLA_HEADS):
            o = o_all[:, h * GLA_DV:(h + 1) * GLA_DV]
            o = o * lax.rsqrt(jnp.mean(o * o, axis=-1, keepdims=True) + LN_EPS) * gng_ref[...]
            gla_blocks.append(o * _silu(r[:, h * GLA_DV:(h + 1) * GLA_DV]))

        mixin = jnp.concatenate([conv_out] + gla_blocks, axis=-1).astype(BF16)
        mix = jnp.concatenate([_dot(mixin, w[...]) for w in wout_refs], axis=1)
        s1_new = _layer_norm(DEEPNORM_ALPHA * s + mix, l1g_ref[...], l1b_ref[...])

        out = _layer_norm(DEEPNORM_ALPHA * s1_prev + f, l2g_ref[...], l2b_ref[...])

        u_new = _dot(sb_new, wmain_ref[...])
        hbuf[0:HALO, :] = halo
        sbuf[...] = state
        s1buf[...] = s1_new
        o_ref[rows_of_tile, :] = out
        resbuf[...] = s_new
        ubuf[...] = u_new
        lgbuf[...] = lg_new
        return carry

    lax.fori_loop(0, PIPE_DEPTH, tile_step, 0)


def _resident(shape):
    return pl.BlockSpec(shape, lambda *_: (0,) * len(shape), pipeline_mode=pl.Buffered(1))


def kernel(x, meta_tokens, ln_in_g, ln_in_b, w_in, conv_w, conv_b, conv_ln_g, conv_ln_b,
           gate_up, gate_bias, gla_norm_g, w_out, ln1_g, ln1_b, w_ff1, w_ff2, ln2_g, ln2_b):
    batch, seq, d_model = x.shape
    assert d_model == D_MODEL and seq % (PIPE_DEPTH * TILE_M) == 0 and w_in.shape[0] == DEPTH
    row = lambda a: a.reshape(1, -1).astype(F32)

    w_main = w_in[0, :, :D_MAIN].astype(BF16)
    w_gd = jnp.pad(w_in[0, :, D_MAIN:], ((0, 0), (0, LANES - GLA_GATE_RANK))).astype(BF16)
    gup = jnp.pad(gate_up[0], ((0, LANES - GLA_GATE_RANK), (0, 0))).astype(BF16)
    lng, lnb, gbias = row(ln_in_g), row(ln_in_b), row(gate_bias[0])

    conv_taps = jnp.broadcast_to(conv_w[0].astype(F32)[:, None, :], (CONV_WIDTH, SUBLANES, D_CONV))

    proj_args = (lng, lnb, w_main, w_gd, gup, gbias)
    h_meta, s_meta = pl.pallas_call(
        _meta_kernel,
        out_shape=(jax.ShapeDtypeStruct((N_META, D_CONV), F32),
                   jax.ShapeDtypeStruct((D_QK, GLA_DV), F32)),
        compiler_params=pltpu.CompilerParams(vmem_limit_bytes=VMEM_LIMIT_BYTES),
        name="meta_state",
    )(meta_tokens.astype(F32), *proj_args)

    layer_args = (
        h_meta, s_meta, *proj_args,
        conv_taps, row(conv_b[0]), row(conv_ln_g[0]), row(conv_ln_b[0]),
        row(gla_norm_g[0]), row(ln1_g[0]), row(ln1_b[0]), row(ln2_g[0]), row(ln2_b[0]),
    )
    big = (w_out[0].astype(F32), w_ff1[0].astype(F32), w_ff2[0].astype(F32))
    big_specs = [pl.BlockSpec(memory_space=pl.ANY) for _ in big]
    big_blocks = [pltpu.VMEM((w.shape[0], W_COLS), BF16) for w in big for _ in range(w.shape[1] // W_COLS)]
    tiles_per_seq = seq // TILE_M
    n_tiles = batch * tiles_per_seq

    blocks_per_seq = tiles_per_seq // PIPE_DEPTH
    n_blocks = batch * blocks_per_seq

    def in_block(g):
        j = jnp.minimum(g, n_blocks - 1)
        return (j // blocks_per_seq, j % blocks_per_seq, 0)

    def out_block(g):
        j = jnp.maximum(g - 1, 0)
        return (j // blocks_per_seq, j % blocks_per_seq, 0)

    return pl.pallas_call(
        functools.partial(_layer_kernel, tiles_per_seq=tiles_per_seq, n_tiles=n_tiles),
        grid=(n_blocks + 1,),
        in_specs=[pl.BlockSpec((None, PIPE_DEPTH * TILE_M, D_MODEL), in_block)]
        + [_resident(a.shape) for a in layer_args] + big_specs,
        out_specs=pl.BlockSpec((None, PIPE_DEPTH * TILE_M, D_MODEL), out_block),
        out_shape=jax.ShapeDtypeStruct(x.shape, x.dtype),
        scratch_shapes=[pltpu.VMEM((HALO + TILE_M, D_CONV), F32),
                        pltpu.VMEM((D_QK, GLA_DV), F32),
                        pltpu.VMEM((TILE_M, D_MODEL), F32),
                        pltpu.VMEM((TILE_M, D_MAIN), F32),
                        pltpu.VMEM((TILE_M, D_QK), F32),
                        pltpu.VMEM((TILE_M, D_MODEL), F32),
                        *big_blocks,
                        pltpu.VMEM((STAGE_SLOTS, STAGE_ROWS, D_MODEL), F32),
                        pltpu.VMEM((STAGE_SLOTS, STAGE_ROWS, D_FF), F32),
                        pltpu.SemaphoreType.DMA((STAGE_SLOTS,))],
        compiler_params=pltpu.CompilerParams(
            dimension_semantics=("arbitrary",),
            vmem_limit_bytes=VMEM_LIMIT_BYTES),
        name="layer",
    )(x, *layer_args, *big)
```
